```python
import jax, jax.numpy as jnp
from jax import lax
import numpy as np

D_MODEL = 1024
BATCH = 4
SEQ = 8192
DEPTH = 4

CHUNK = 64
Q_BLOCK = 128
PLE_DIM = 256
GDN_HEADS = 4
GDN_DK = 128
GDN_DV = 128
CONV_WIDTH = 4
MLA_HEADS = 4
MLA_NOPE = 128
MLA_ROPE = 64
MLA_V = 128
Q_LORA = 384
KV_LORA = 256
ROPE_THETA = 10000.0
D_FF = -(-(8 * D_MODEL) // (3 * 256)) * 256

GDN_QK = GDN_HEADS * GDN_DK
GDN_VW = GDN_HEADS * GDN_DV
MIX_WIDTH = GDN_VW + MLA_HEADS * MLA_V
IN_SIZES = (GDN_QK, GDN_QK, GDN_VW, GDN_VW, GDN_HEADS, GDN_HEADS, Q_LORA, KV_LORA, MLA_ROPE)
IN_SPLITS = tuple(int(v) for v in np.cumsum(IN_SIZES)[:-1])
IN_WIDTH = int(sum(IN_SIZES))
ALPHA = (2.0 * DEPTH) ** 0.25
BETA = (8.0 * DEPTH) ** -0.25
LN_EPS = 1e-5
RMS_EPS = 1e-6

kernel_name = 'hymba_gdn_mla_deepnorm_ple_trunk'


def layer_norm(x, g, b):
    xf = x.astype(jnp.float32)
    mu = jnp.mean(xf, -1, keepdims=True)
    var = jnp.mean(jnp.square(xf - mu), -1, keepdims=True)
    return ((xf - mu) * lax.rsqrt(var + LN_EPS) * g + b).astype(x.dtype)


def rms_norm(x, g):
    xf = x.astype(jnp.float32)
    return (xf * lax.rsqrt(jnp.mean(xf * xf, -1, keepdims=True) + RMS_EPS) * g).astype(x.dtype)


def l2_norm(x):
    xf = x.astype(jnp.float32)
    return xf * lax.rsqrt(jnp.sum(xf * xf, -1, keepdims=True) + RMS_EPS)


def rope_tables(positions):
    inv_freq = ROPE_THETA ** (-jnp.arange(0, MLA_ROPE, 2, dtype=jnp.float32) / MLA_ROPE)
    ang = positions.astype(jnp.float32)[..., None] * inv_freq
    return jnp.cos(ang), jnp.sin(ang)


def apply_rope(x, cos, sin):
    x1, x2 = jnp.split(x.astype(jnp.float32), 2, axis=-1)
    return jnp.concatenate([x1 * cos - x2 * sin, x2 * cos + x1 * sin], -1).astype(x.dtype)


def causal_dwconv(x, w):
    return lax.conv_general_dilated(
        x, w[:, None, :].astype(x.dtype), window_strides=(1,),
        padding=[(CONV_WIDTH - 1, 0)], dimension_numbers=('NWC', 'WIO', 'NWC'),
        feature_group_count=x.shape[-1])


def gated_delta_rule(q, k, v, g, beta):
    B, S, H, DK = q.shape
    DV = v.shape[-1]
    N = S // CHUNK
    f32 = jnp.float32

    def chunks(t):
        t = t.astype(f32).reshape((B, N, CHUNK, H) + t.shape[3:])
        return jnp.moveaxis(t, (1, 3), (0, 2))

    q = chunks(q) * DK ** -0.5
    k = chunks(k)
    v = chunks(v)
    beta = chunks(beta)
    g = jnp.cumsum(chunks(g), axis=-1)
    tri_incl = jnp.tril(jnp.ones((CHUNK, CHUNK), bool))
    tri_strict = jnp.tril(jnp.ones((CHUNK, CHUNK), bool), -1)
    decay = jnp.exp(jnp.where(tri_incl, g[..., :, None] - g[..., None, :], -jnp.inf))
    k_beta = k * beta[..., None]
    lower = jnp.where(tri_strict, jnp.einsum('nbhid,nbhjd->nbhij', k_beta, k) * decay, 0.0)
    rhs = jnp.concatenate([v * beta[..., None], k_beta * jnp.exp(g)[..., None]], -1)
    sol = lax.linalg.triangular_solve(lower + jnp.eye(CHUNK, dtype=f32), rhs,
                                      left_side=True, lower=True, unit_diagonal=True)
    u, w = sol[..., :DV], sol[..., DV:]
    attn = jnp.where(tri_incl, jnp.einsum('nbhid,nbhjd->nbhij', q, k) * decay, 0.0)
    g_last = g[..., -1]
    q_dec = q * jnp.exp(g)[..., None]
    k_dec = k * jnp.exp(g_last[..., None] - g)[..., None]

    def step(state, xs):
        q_c, k_c, u_c, w_c, attn_c, gl = xs
        v_new = u_c - jnp.einsum('bhck,bhkv->bhcv', w_c, state)
        o = jnp.einsum('bhck,bhkv->bhcv', q_c, state) + jnp.einsum('bhij,bhjv->bhiv', attn_c, v_new)
        state = state * jnp.exp(gl)[..., None, None] + jnp.einsum('bhck,bhcv->bhkv', k_c, v_new)
        return state, o

    s0 = jnp.zeros((B, H, DK, DV), f32)
    _, o = lax.scan(step, s0, (q_dec, k_dec, u, w, attn, g_last))
    return jnp.moveaxis(o, (0, 2), (1, 3)).reshape(B, S, H, DV)


def mla_attention(q_nope, q_rope, k_nope, k_rope, v):
    S = q_nope.shape[1]
    scale = (MLA_NOPE + MLA_ROPE) ** -0.5
    frame_chunk = jnp.arange(S) // CHUNK
    outs = []
    for blk in range(S // Q_BLOCK):
        q0, q1 = blk * Q_BLOCK, (blk + 1) * Q_BLOCK
        s = (jnp.einsum('bqhd,bkhd->bhqk', q_nope[:, q0:q1], k_nope[:, :q1])
             + jnp.einsum('bqhd,bkd->bhqk', q_rope[:, q0:q1], k_rope[:, :q1])).astype(jnp.float32) * scale
        mask = frame_chunk[None, :q1] <= frame_chunk[q0:q1, None]
        prob = jax.nn.softmax(jnp.where(mask, s, -jnp.inf), axis=-1).astype(v.dtype)
        outs.append(jnp.einsum('bhqk,bkhd->bqhd', prob, v[:, :q1]))
    return jnp.concatenate(outs, axis=1)


def hybrid_layer(x, p_i, cos, sin, w_in, conv_w, a_log, dt_bias, gdn_norm_g, q_norm_g,
                 w_uq, kv_norm_g, w_ukv, w_out, ln1_g, ln1_b, w_gate_up, w_down,
                 ln2_g, ln2_b, w_ple, w_ple_gate):
    B, S, _ = x.shape
    h = x @ w_in
    q, k, v, z, b, a, c_q, c_kv, k_r = jnp.split(h, IN_SPLITS, axis=-1)

    qkv = jax.nn.silu(causal_dwconv(jnp.concatenate([q, k, v], -1), conv_w))
    q, k, v = jnp.split(qkv, [GDN_QK, 2 * GDN_QK], axis=-1)
    q = l2_norm(q.reshape(B, S, GDN_HEADS, GDN_DK))
    k = l2_norm(k.reshape(B, S, GDN_HEADS, GDN_DK))
    v = v.reshape(B, S, GDN_HEADS, GDN_DV)
    beta = jax.nn.sigmoid(b.astype(jnp.float32))
    g = -jnp.exp(a_log.astype(jnp.float32)) * jax.nn.softplus(a.astype(jnp.float32) + dt_bias)
    o_gdn = gated_delta_rule(q, k, v, g, beta)
    o_gdn = rms_norm(o_gdn, gdn_norm_g) * jax.nn.silu(z.reshape(B, S, GDN_HEADS, GDN_DV).astype(jnp.float32))

    qm = (rms_norm(c_q, q_norm_g) @ w_uq).reshape(B, S, MLA_HEADS, MLA_NOPE + MLA_ROPE)
    q_nope, q_rope = jnp.split(qm, [MLA_NOPE], axis=-1)
    q_rope = apply_rope(q_rope, cos[:, :, None, :], sin[:, :, None, :])
    kv = (rms_norm(c_kv, kv_norm_g) @ w_ukv).reshape(B, S, MLA_HEADS, MLA_NOPE + MLA_V)
    k_nope, v_m = jnp.split(kv, [MLA_NOPE], axis=-1)
    k_rope = apply_rope(k_r, cos, sin)
    o_mla = mla_attention(q_nope, q_rope, k_nope, k_rope, v_m)

    mix = jnp.concatenate([o_gdn.reshape(B, S, GDN_VW).astype(x.dtype),
                           o_mla.reshape(B, S, MLA_HEADS * MLA_V)], axis=-1) @ w_out
    x = layer_norm(ALPHA * x + mix, ln1_g, ln1_b)

    gate, up = jnp.split(x @ w_gate_up, 2, axis=-1)
    x = layer_norm(ALPHA * x + (jax.nn.silu(gate) * up) @ w_down, ln2_g, ln2_b)

    return x + jax.nn.sigmoid(x @ w_ple_gate) * (p_i @ w_ple)


def setup_inputs(seed: int = 0) -> dict:
    key = jax.random.key(seed)
    ks = jax.random.split(key, 24)
    f32 = jnp.float32
    nrm = lambda k, shape, scale: jax.random.normal(k, shape, f32) * scale
    x = jax.random.normal(ks[0], (BATCH, SEQ, D_MODEL), f32)
    p = jax.random.normal(ks[1], (DEPTH, BATCH, SEQ, PLE_DIM), f32)
    offset = jax.random.randint(ks[2], (BATCH, 1), 0, 64) * CHUNK
    positions = (offset + jnp.arange(SEQ, dtype=jnp.int32)[None, :]).astype(jnp.int32)
    dt = jnp.exp(jax.random.uniform(ks[3], (DEPTH, GDN_HEADS), f32) * (np.log(0.1) - np.log(1e-3)) + np.log(1e-3))
    return {
        'x': x,
        'p': p,
        'positions': positions,
        'w_in': nrm(ks[4], (DEPTH, D_MODEL, IN_WIDTH), D_MODEL ** -0.5),
        'conv_w': nrm(ks[5], (DEPTH, CONV_WIDTH, 2 * GDN_QK + GDN_VW), CONV_WIDTH ** -0.5),
        'a_log': jnp.log(jax.random.uniform(ks[6], (DEPTH, GDN_HEADS), f32, 1.0, 16.0)),
        'dt_bias': dt + jnp.log(-jnp.expm1(-dt)),
        'gdn_norm_g': 1.0 + nrm(ks[7], (DEPTH, GDN_DV), 0.1),
        'q_norm_g': 1.0 + nrm(ks[8], (DEPTH, Q_LORA), 0.1),
        'w_uq': nrm(ks[9], (DEPTH, Q_LORA, MLA_HEADS * (MLA_NOPE + MLA_ROPE)), Q_LORA ** -0.5),
        'kv_norm_g': 1.0 + nrm(ks[10], (DEPTH, KV_LORA), 0.1),
        'w_ukv': nrm(ks[11], (DEPTH, KV_LORA, MLA_HEADS * (MLA_NOPE + MLA_V)), KV_LORA ** -0.5),
        'w_out': nrm(ks[12], (DEPTH, MIX_WIDTH, D_MODEL), MIX_WIDTH ** -0.5 * BETA),
        'ln1_g': 1.0 + nrm(ks[13], (DEPTH, D_MODEL), 0.1),
        'ln1_b': nrm(ks[14], (DEPTH, D_MODEL), 0.02),
        'w_gate_up': nrm(ks[15], (DEPTH, D_MODEL, 2 * D_FF), D_MODEL ** -0.5),
        'w_down': nrm(ks[16], (DEPTH, D_FF, D_MODEL), D_FF ** -0.5 * BETA),
        'ln2_g': 1.0 + nrm(ks[17], (DEPTH, D_MODEL), 0.1),
        'ln2_b': nrm(ks[18], (DEPTH, D_MODEL), 0.02),
        'w_ple': nrm(ks[19], (DEPTH, PLE_DIM, D_MODEL), PLE_DIM ** -0.5),
        'w_ple_gate': nrm(ks[20], (DEPTH, D_MODEL, D_MODEL), D_MODEL ** -0.5),
    }


def reference(x, p, positions, w_in, conv_w, a_log, dt_bias, gdn_norm_g, q_norm_g, w_uq,
              kv_norm_g, w_ukv, w_out, ln1_g, ln1_b, w_gate_up, w_down, ln2_g, ln2_b,
              w_ple, w_ple_gate):
    cos, sin = rope_tables(positions)
    for i in range(DEPTH):
        x = hybrid_layer(x, p[i], cos, sin, w_in[i], conv_w[i], a_log[i], dt_bias[i],
                         gdn_norm_g[i], q_norm_g[i], w_uq[i], kv_norm_g[i], w_ukv[i],
                         w_out[i], ln1_g[i], ln1_b[i], w_gate_up[i], w_down[i],
                         ln2_g[i], ln2_b[i], w_ple[i], w_ple_gate[i])
    return x
```

```python
import functools

import jax
import jax.numpy as jnp
import numpy as np
from jax import lax
from jax.experimental import pallas as pl
from jax.experimental.pallas import tpu as pltpu

F32 = jnp.float32
BF16 = jnp.bfloat16

CHUNK = 64
GDN_HEADS = 4
GDN_DK = 128
GDN_DV = 128
CONV_WIDTH = 4
MLA_HEADS = 4
MLA_NOPE = 128
MLA_ROPE = 64
MLA_V = 128
Q_LORA = 384
KV_LORA = 256
ROPE_THETA = 10000.0
LN_EPS = 1e-5
RMS_EPS = 1e-6

GDN_QK = GDN_HEADS * GDN_DK
GDN_VW = GDN_HEADS * GDN_DV
QKV_W = 2 * GDN_QK + GDN_VW
LANES = 128
Z_OFF = QKV_W
CQ_OFF = Z_OFF + GDN_VW
CKV_OFF = CQ_OFF + Q_LORA
LAST_OFF = CKV_OFF + KV_LORA
IN_PAD = LAST_OFF + LANES
BETA_LANE = MLA_ROPE
G_LANE = MLA_ROPE + GDN_HEADS
ATT_DK = 2 * LANES

V7X_VMEM_LIMIT = 56 * 1024 * 1024

HIGHEST = lax.Precision.HIGHEST


def _dot(a, b, precision=None):
    return jnp.dot(a, b, preferred_element_type=F32, precision=precision)


def _dot_nt(a, b, precision=None):
    return lax.dot_general(a, b, (((1,), (1,)), ((), ())), preferred_element_type=F32,
                           precision=precision)


def _dot_tn(a, b, precision=None):
    return lax.dot_general(a, b, (((0,), (0,)), ((), ())), preferred_element_type=F32,
                           precision=precision)


def _sigmoid(x):
    return 1.0 / (1.0 + jnp.exp(-x))


def _silu(x):
    return x * _sigmoid(x)


def _softplus(x):
    return jnp.maximum(x, 0.0) + jnp.log(1.0 + jnp.exp(-jnp.abs(x)))


def _resident(shape):
    nd = len(shape)
    return pl.BlockSpec(shape, lambda *_: (0,) * nd, pipeline_mode=pl.Buffered(1))


def _prep_kernel(x_ref, w_in_ref, conv_w_ref, alog_ref, dtb_ref, qng_ref, kvng_ref, w_uq_ref,
                 w_ukv_ref, cos_ref, sin_ref,
                 qg_ref, kg_ref, vg_ref, zg_ref, gbcol_ref, gbrow_ref, qa_ref, ka_ref, vt_ref,
                 cbuf, *, tm, att_scale):
    i = pl.program_id(1)
    h = _dot(x_ref[0].astype(BF16), w_in_ref[...])

    @pl.when(i == 0)
    def _():
        cbuf[0:8, :] = jnp.zeros((8, QKV_W), F32)

    qkv = h[:, :QKV_W]
    cbuf[8:tm + 8, :] = qkv
    cw = conv_w_ref[...]
    y = (cbuf[5:tm + 5, :] * cw[0:1, :] + cbuf[6:tm + 6, :] * cw[1:2, :]
         + cbuf[7:tm + 7, :] * cw[2:3, :] + qkv * cw[3:4, :])
    cbuf[0:8, :] = cbuf[tm:tm + 8, :]
    y = _silu(y)

    for hd in range(GDN_HEADS):
        lo = hd * GDN_DK
        qh = y[:, lo:lo + GDN_DK]
        kh = y[:, GDN_QK + lo:GDN_QK + lo + GDN_DK]
        qn = qh * lax.rsqrt(jnp.sum(qh * qh, -1, keepdims=True) + RMS_EPS)
        kn = kh * lax.rsqrt(jnp.sum(kh * kh, -1, keepdims=True) + RMS_EPS)
        qg_ref[0, :, lo:lo + GDN_DK] = (qn * GDN_DK ** -0.5).astype(BF16)
        kg_ref[0, :, lo:lo + GDN_DK] = kn.astype(BF16)
    vg_ref[0] = y[:, 2 * GDN_QK:QKV_W].astype(BF16)
    zg_ref[0] = h[:, Z_OFF:Z_OFF + GDN_VW].astype(BF16)

    last = h[:, LAST_OFF:IN_PAD]
    lane = lax.broadcasted_iota(jnp.int32, (tm, LANES), 1)
    beta = _sigmoid(last)
    g = -jnp.exp(alog_ref[...]) * _softplus(last + dtb_ref[...])
    gb = jnp.where((lane >= BETA_LANE) & (lane < G_LANE), beta,
                   jnp.where((lane >= G_LANE) & (lane < G_LANE + GDN_HEADS), g, 0.0))
    gbcol_ref[0] = gb
    gbrow_ref[0] = gb.T[BETA_LANE:BETA_LANE + 2 * GDN_HEADS, :]

    cos = cos_ref[0]
    sin = sin_ref[0]
    first_half = (lane % MLA_ROPE) < (MLA_ROPE // 2)

    def rope(blk):
        partner = jnp.where(first_half, pltpu.roll(blk, LANES - MLA_ROPE // 2, 1),
                            pltpu.roll(blk, MLA_ROPE // 2, 1))
        return blk * cos + partner * sin

    cq = h[:, CQ_OFF:CQ_OFF + Q_LORA]
    cqn = cq * lax.rsqrt(jnp.mean(cq * cq, -1, keepdims=True) + RMS_EPS) * qng_ref[...]
    qm = _dot(cqn.astype(BF16), w_uq_ref[...])
    ckv = h[:, CKV_OFF:CKV_OFF + KV_LORA]
    ckvn = ckv * lax.rsqrt(jnp.mean(ckv * ckv, -1, keepdims=True) + RMS_EPS) * kvng_ref[...]
    kvm = _dot(ckvn.astype(BF16), w_ukv_ref[...])
    k_rope = jnp.where(lane < MLA_ROPE, rope(last), 0.0).astype(BF16)
    nh = MLA_HEADS * MLA_NOPE
    for hd in range(MLA_HEADS):
        lo = hd * LANES
        qa_ref[0, hd, :, 0:LANES] = (qm[:, lo:lo + LANES] * att_scale).astype(BF16)
        qa_ref[0, hd, :, LANES:ATT_DK] = (rope(qm[:, nh + lo:nh + lo + LANES]) * att_scale).astype(BF16)
        ka_ref[0, hd, :, 0:LANES] = kvm[:, lo:lo + LANES].astype(BF16)
        ka_ref[0, hd, :, LANES:ATT_DK] = k_rope
        vt_ref[0, hd, 0] = kvm[:, nh + lo:nh + lo + LANES].T.astype(BF16)


def _prep_call(x, w_in_r, conv_w, alog_pad, dtb_pad, qn_g, kvn_g, w_uq_r, w_ukv_r, cos_t, sin_t,
               *, tm):
    B, S, D = x.shape
    nt = S // tm
    row = lambda w: pl.BlockSpec((1, tm, w), lambda b, i: (b, i, 0))
    head = pl.BlockSpec((1, MLA_HEADS, tm, ATT_DK), lambda b, i: (b, 0, i, 0))
    out_shape = (
        jax.ShapeDtypeStruct((B, S, GDN_QK), BF16), jax.ShapeDtypeStruct((B, S, GDN_QK), BF16),
        jax.ShapeDtypeStruct((B, S, GDN_VW), BF16), jax.ShapeDtypeStruct((B, S, GDN_VW), BF16),
        jax.ShapeDtypeStruct((B, S, LANES), F32), jax.ShapeDtypeStruct((B, 2 * GDN_HEADS, S), F32),
        jax.ShapeDtypeStruct((B, MLA_HEADS, S, ATT_DK), BF16),
        jax.ShapeDtypeStruct((B, MLA_HEADS, S, ATT_DK), BF16),
        jax.ShapeDtypeStruct((B, MLA_HEADS, nt, MLA_V, tm), BF16),
    )
    out_specs = (
        row(GDN_QK), row(GDN_QK), row(GDN_VW), row(GDN_VW), row(LANES),
        pl.BlockSpec((1, 2 * GDN_HEADS, tm), lambda b, i: (b, 0, i)),
        head, head,
        pl.BlockSpec((1, MLA_HEADS, 1, MLA_V, tm), lambda b, i: (b, 0, i, 0, 0)),
    )
    in_specs = [
        row(D), _resident(w_in_r.shape), _resident(conv_w.shape), _resident(alog_pad.shape),
        _resident(dtb_pad.shape), _resident(qn_g.shape), _resident(kvn_g.shape),
        _resident(w_uq_r.shape), _resident(w_ukv_r.shape), row(LANES), row(LANES),
    ]
    kern = functools.partial(_prep_kernel, tm=tm, att_scale=(MLA_NOPE + MLA_ROPE) ** -0.5)
    return pl.pallas_call(
        kern, out_shape=out_shape, grid=(B, nt), in_specs=in_specs, out_specs=out_specs,
        scratch_shapes=[pltpu.VMEM((tm + 8, QKV_W), F32)],
        compiler_params=pltpu.CompilerParams(dimension_semantics=("arbitrary", "arbitrary"),
                                             vmem_limit_bytes=V7X_VMEM_LIMIT),
        name="prep",
    )(x, w_in_r, conv_w, alog_pad, dtb_pad, qn_g, kvn_g, w_uq_r, w_ukv_r, cos_t, sin_t)


def _unit_lower_inverse(low, eye):
    inv = eye - low
    pw = low
    n = low.shape[0]
    k = 2
    while k < n:
        pw = _dot(pw, pw, HIGHEST)
        inv = inv + _dot(inv, pw, HIGHEST)
        k *= 2
    return inv


def _gdn_kernel(q_ref, k_ref, v_ref, z_ref, gbcol_ref, gbrow_ref, ng_ref, o_ref, state, *, nchunk):
    @pl.when(pl.program_id(1) == 0)
    def _():
        state[...] = jnp.zeros_like(state)

    C = CHUNK
    r = lax.broadcasted_iota(jnp.int32, (C, C), 0)
    c = lax.broadcasted_iota(jnp.int32, (C, C), 1)
    tri_incl = r >= c
    tri_strict = r > c
    eye = (r == c).astype(F32)
    lower_ones = tri_incl.astype(F32)
    upper_ones = (r <= c).astype(F32)
    ng = ng_ref[...]

    for ci in range(nchunk):
        r0 = ci * C
        gbc = gbcol_ref[0, r0:r0 + C, :]
        gcum_c = _dot(lower_ones, gbc, HIGHEST)
        gcum_r = _dot(gbrow_ref[0, :, r0:r0 + C], upper_ones, HIGHEST)
        for hd in range(GDN_HEADS):
            lo = hd * GDN_DK
            gc = gcum_c[:, G_LANE + hd:G_LANE + hd + 1]
            gr = gcum_r[GDN_HEADS + hd:GDN_HEADS + hd + 1, :]
            gl = gc[C - 1:C, :]
            beta = gbc[:, BETA_LANE + hd:BETA_LANE + hd + 1]
            decay = jnp.exp(jnp.where(tri_incl, gc - gr, -jnp.inf))
            eg = jnp.exp(gc)
            qh = q_ref[0, r0:r0 + C, lo:lo + GDN_DK].astype(F32)
            kh = k_ref[0, r0:r0 + C, lo:lo + GDN_DK].astype(F32)
            vh = v_ref[0, r0:r0 + C, lo:lo + GDN_DV].astype(F32)
            kb = kh * beta
            low = jnp.where(tri_strict, _dot_nt(kb, kh) * decay, 0.0)
            tinv = _unit_lower_inverse(low, eye)
            u = _dot(tinv, vh * beta)
            w = _dot(tinv, kb * eg)
            attn = jnp.where(tri_incl, _dot_nt(qh, kh) * decay, 0.0)
            s = state[hd]
            v_new = u - _dot(w, s)
            o = _dot(qh * eg, s) + _dot(attn, v_new)
            state[hd] = s * jnp.exp(gl) + _dot_tn(kh * jnp.exp(gl - gc), v_new)
            zh = z_ref[0, r0:r0 + C, lo:lo + GDN_DV].astype(F32)
            on = o * lax.rsqrt(jnp.mean(o * o, -1, keepdims=True) + RMS_EPS) * ng
            o_ref[0, r0:r0 + C, lo:lo + GDN_DV] = (on * _silu(zh)).astype(BF16)


def _gdn_call(qg, kg, vg, zg, gbcol, gbrow, norm_g, *, rows):
    B, S, _ = qg.shape
    row = lambda w: pl.BlockSpec((1, rows, w), lambda b, i: (b, i, 0))
    kern = functools.partial(_gdn_kernel, nchunk=rows // CHUNK)
    return pl.pallas_call(
        kern, out_shape=jax.ShapeDtypeStruct((B, S, GDN_VW), BF16), grid=(B, S // rows),
        in_specs=[row(GDN_QK), row(GDN_QK), row(GDN_VW), row(GDN_VW), row(LANES),
                  pl.BlockSpec((1, 2 * GDN_HEADS, rows), lambda b, i: (b, 0, i)),
                  _resident(norm_g.shape)],
        out_specs=row(GDN_VW),
        scratch_shapes=[pltpu.VMEM((GDN_HEADS, GDN_DK, GDN_DV), F32)],
        compiler_params=pltpu.CompilerParams(dimension_semantics=("arbitrary", "arbitrary"),
                                             vmem_limit_bytes=V7X_VMEM_LIMIT),
        name="gdn",
    )(qg, kg, vg, zg, gbcol, gbrow, norm_g)


def _attn_kernel(q_ref, k_ref, vt_ref, o_ref, m_ref, l_ref, acc_ref, *, blk):
    i = pl.program_id(2)
    q = q_ref[0, 0]
    m_ref[...] = jnp.full_like(m_ref, -jnp.inf)
    l_ref[...] = jnp.zeros_like(l_ref)
    acc_ref[...] = jnp.zeros_like(acc_ref)

    def step(j, masked):
        start = pl.multiple_of(j * blk, blk)
        st = _dot_nt(k_ref[0, 0, pl.ds(start, blk), :], q)
        if masked:
            kc = lax.broadcasted_iota(jnp.int32, (blk, blk), 0) // CHUNK
            qc = lax.broadcasted_iota(jnp.int32, (blk, blk), 1) // CHUNK
            st = jnp.where(kc <= qc, st, -jnp.inf)
        m_old = m_ref[...]
        m_new = jnp.maximum(m_old, jnp.max(st, axis=0, keepdims=True))
        alpha = jnp.exp(m_old - m_new)
        p = jnp.exp(st - m_new)
        l_ref[...] = alpha * l_ref[...] + jnp.sum(p, axis=0, keepdims=True)
        acc_ref[...] = alpha * acc_ref[...] + _dot(vt_ref[0, 0, j], p.astype(BF16))
        m_ref[...] = m_new

    step(i, True)
    lax.fori_loop(0, i, lambda j, carry: (step(j, False), carry)[1], 0)
    o_ref[0] = (acc_ref[...] / l_ref[...]).T.astype(BF16)


def _attn_call(qa, ka, vt, *, blk):
    B, H, S, _ = qa.shape
    nb = S // blk
    kern = functools.partial(_attn_kernel, blk=blk)
    return pl.pallas_call(
        kern, out_shape=jax.ShapeDtypeStruct((B, S, H * MLA_V), BF16), grid=(B, H, nb),
        in_specs=[pl.BlockSpec((1, 1, blk, ATT_DK), lambda b, h, i: (b, h, i, 0)),
                  pl.BlockSpec((1, 1, S, ATT_DK), lambda b, h, i: (b, h, 0, 0)),
                  pl.BlockSpec((1, 1, nb, MLA_V, blk), lambda b, h, i: (b, h, 0, 0, 0))],
        out_specs=pl.BlockSpec((1, blk, MLA_V), lambda b, h, i: (b, i, h)),
        scratch_shapes=[pltpu.VMEM((1, blk), F32), pltpu.VMEM((1, blk), F32),
                        pltpu.VMEM((MLA_V, blk), F32)],
        compiler_params=pltpu.CompilerParams(
            dimension_semantics=("arbitrary", "arbitrary", "arbitrary"),
            vmem_limit_bytes=V7X_VMEM_LIMIT),
        name="attn",
    )(qa, ka, vt)


def _layer_norm(x, g, b):
    mu = jnp.mean(x, -1, keepdims=True)
    xc = x - mu
    var = jnp.mean(xc * xc, -1, keepdims=True)
    return xc * lax.rsqrt(var + LN_EPS) * g + b


def _post_kernel(x_ref, og_ref, om_ref, p_ref, wog_ref, wom_ref, l1g_ref, l1b_ref, wg_ref, wu_ref,
                 wd_ref, l2g_ref, l2b_ref, wple_ref, wpg_ref, o_ref, *, alpha, ff_chunk):
    mix = _dot(og_ref[...], wog_ref[...]) + _dot(om_ref[...], wom_ref[...])
    x1 = _layer_norm(alpha * x_ref[...] + mix, l1g_ref[...], l1b_ref[...])
    x1b = x1.astype(BF16)
    d_ff = wd_ref.shape[0]
    ff = jnp.zeros_like(x1)
    for c0 in range(0, d_ff, ff_chunk):
        gate = _dot(x1b, wg_ref[:, c0:c0 + ff_chunk])
        up = _dot(x1b, wu_ref[:, c0:c0 + ff_chunk])
        ff = ff + _dot((_silu(gate) * up).astype(BF16), wd_ref[c0:c0 + ff_chunk, :])
    x2 = _layer_norm(alpha * x1 + ff, l2g_ref[...], l2b_ref[...])
    emb = _dot(p_ref[...].astype(BF16), wple_ref[...])
    o_ref[...] = x2 + _sigmoid(_dot(x2.astype(BF16), wpg_ref[...])) * emb


def _post_call(x2d, og, om, p2d, w_out_g, w_out_m, ln1_g, ln1_b, w_gate, w_up, w_down, ln2_g, ln2_b,
               w_ple, w_pg, *, tm, alpha, ff_chunk):
    T, D = x2d.shape
    row = lambda w: pl.BlockSpec((tm, w), lambda i: (i, 0))
    weights = (w_out_g, w_out_m, ln1_g, ln1_b, w_gate, w_up, w_down, ln2_g, ln2_b, w_ple, w_pg)
    kern = functools.partial(_post_kernel, alpha=alpha, ff_chunk=ff_chunk)
    return pl.pallas_call(
        kern, out_shape=jax.ShapeDtypeStruct((T, D), F32), grid=(T // tm,),
        in_specs=[row(D), row(og.shape[1]), row(om.shape[1]), row(p2d.shape[1])]
        + [_resident(w.shape) for w in weights],
        out_specs=row(D),
        compiler_params=pltpu.CompilerParams(dimension_semantics=("arbitrary",),
                                             vmem_limit_bytes=V7X_VMEM_LIMIT),
        name="post",
    )(x2d, og, om, p2d, *weights)


def _reorder_weights(w_in, w_uq, w_ukv, a_log, dt_bias):
    depth, d, _ = w_in.shape
    q, k, v, z, b, a, cq, ckv, kr = jnp.split(
        w_in, np.cumsum((GDN_QK, GDN_QK, GDN_VW, GDN_VW, GDN_HEADS, GDN_HEADS, Q_LORA, KV_LORA))
        .tolist(), axis=-1)
    pad = jnp.zeros((depth, d, LANES - MLA_ROPE - 2 * GDN_HEADS), w_in.dtype)
    w_in_r = jnp.concatenate([q, k, v, z, cq, ckv, kr, b, a, pad], -1).astype(BF16)

    uq = w_uq.reshape(depth, Q_LORA, MLA_HEADS, MLA_NOPE + MLA_ROPE)
    uq_nope = uq[..., :MLA_NOPE].reshape(depth, Q_LORA, MLA_HEADS * MLA_NOPE)
    uq_rope = jnp.pad(uq[..., MLA_NOPE:], ((0, 0), (0, 0), (0, 0), (0, LANES - MLA_ROPE)))
    w_uq_r = jnp.concatenate([uq_nope, uq_rope.reshape(depth, Q_LORA, MLA_HEADS * LANES)],
                             -1).astype(BF16)

    ukv = w_ukv.reshape(depth, KV_LORA, MLA_HEADS, MLA_NOPE + MLA_V)
    w_ukv_r = jnp.concatenate([ukv[..., :MLA_NOPE].reshape(depth, KV_LORA, -1),
                               ukv[..., MLA_NOPE:].reshape(depth, KV_LORA, -1)], -1).astype(BF16)

    lanes_pad = ((0, 0), (0, 0), (G_LANE, LANES - G_LANE - GDN_HEADS))
    alog_pad = jnp.pad(a_log[:, None, :], lanes_pad)
    dtb_pad = jnp.pad(dt_bias[:, None, :], lanes_pad)
    return w_in_r, w_uq_r, w_ukv_r, alog_pad, dtb_pad


def _rope_tables(positions):
    inv_freq = ROPE_THETA ** (-jnp.arange(0, MLA_ROPE, 2, dtype=F32) / MLA_ROPE)
    ang = positions.astype(F32)[..., None] * inv_freq
    cos, sin = jnp.cos(ang), jnp.sin(ang)
    reps = LANES // MLA_ROPE
    return (jnp.tile(jnp.concatenate([cos, cos], -1), (1, 1, reps)),
            jnp.tile(jnp.concatenate([-sin, sin], -1), (1, 1, reps)))


def _tile_rows(s, want):
    t = min(want, s)
    while s % t:
        t //= 2
    return t


def kernel(x, p, positions, w_in, conv_w, a_log, dt_bias, gdn_norm_g, q_norm_g, w_uq, kv_norm_g,
           w_ukv, w_out, ln1_g, ln1_b, w_gate_up, w_down, ln2_g, ln2_b, w_ple, w_ple_gate):
    B, S, D = x.shape
    depth = w_in.shape[0]
    d_ff = w_down.shape[1]
    alpha = (2.0 * depth) ** 0.25
    tm = _tile_rows(S, 512)
    gdn_rows = _tile_rows(S, 256)
    ff_chunk = d_ff // 2 if (d_ff // 2) % LANES == 0 else d_ff

    w_in_r, w_uq_r, w_ukv_r, alog_pad, dtb_pad = _reorder_weights(w_in, w_uq, w_ukv, a_log, dt_bias)
    cos_t, sin_t = _rope_tables(positions)
    w_out_b = w_out.astype(BF16)
    w_gu_b = w_gate_up.astype(BF16)
    w_down_b = w_down.astype(BF16)
    w_ple_b = w_ple.astype(BF16)
    w_pg_b = w_ple_gate.astype(BF16)
    vec = lambda a: a[:, None, :]

    for i in range(depth):
        qg, kg, vg, zg, gbcol, gbrow, qa, ka, vt = _prep_call(
            x, w_in_r[i], conv_w[i], alog_pad[i], dtb_pad[i], vec(q_norm_g)[i], vec(kv_norm_g)[i],
            w_uq_r[i], w_ukv_r[i], cos_t, sin_t, tm=tm)
        og = _gdn_call(qg, kg, vg, zg, gbcol, gbrow, vec(gdn_norm_g)[i], rows=gdn_rows)
        om = _attn_call(qa, ka, vt, blk=tm)
        x = _post_call(
            x.reshape(B * S, D), og.reshape(B * S, -1), om.reshape(B * S, -1),
            p[i].reshape(B * S, -1), w_out_b[i, :GDN_VW], w_out_b[i, GDN_VW:], vec(ln1_g)[i],
            vec(ln1_b)[i], w_gu_b[i, :, :d_ff], w_gu_b[i, :, d_ff:], w_down_b[i], vec(ln2_g)[i],
            vec(ln2_b)[i], w_ple_b[i], w_pg_b[i], tm=tm, alpha=alpha, ff_chunk=ff_chunk,
        ).reshape(B, S, D)
    return x
```

```python
import functools

import jax
import jax.numpy as jnp
import numpy as np
from jax import lax
from jax.experimental import pallas as pl
from jax.experimental.pallas import tpu as pltpu

F32 = jnp.float32
BF16 = jnp.bfloat16

CHUNK = 64
GDN_HEADS = 4
GDN_DK = 128
GDN_DV = 128
CONV_WIDTH = 4
MLA_HEADS = 4
MLA_NOPE = 128
MLA_ROPE = 64
MLA_V = 128
Q_LORA = 384
KV_LORA = 256
ROPE_THETA = 10000.0
LN_EPS = 1e-5
RMS_EPS = 1e-6

GDN_QK = GDN_HEADS * GDN_DK
GDN_VW = GDN_HEADS * GDN_DV
QKV_W = 2 * GDN_QK + GDN_VW
LANES = 128
Z_OFF = QKV_W
CQ_OFF = Z_OFF + GDN_VW
CKV_OFF = CQ_OFF + Q_LORA
LAST_OFF = CKV_OFF + KV_LORA
IN_PAD = LAST_OFF + LANES
BETA_LANE = MLA_ROPE
G_LANE = MLA_ROPE + GDN_HEADS
ATT_DK = 2 * LANES

V7X_VMEM_LIMIT = 56 * 1024 * 1024

HIGHEST = lax.Precision.HIGHEST


def _dot(a, b, precision=None):
    return jnp.dot(a, b, preferred_element_type=F32, precision=precision)


def _dot_nt(a, b, precision=None):
    return lax.dot_general(a, b, (((1,), (1,)), ((), ())), preferred_element_type=F32,
                           precision=precision)


def _dot_tn(a, b, precision=None):
    return lax.dot_general(a, b, (((0,), (0,)), ((), ())), preferred_element_type=F32,
                           precision=precision)


def _bdot(a, b):
    return _dot(a.astype(BF16), b.astype(BF16))


def _bdot_nt(a, b):
    return _dot_nt(a.astype(BF16), b.astype(BF16))


def _bdot_tn(a, b):
    return _dot_tn(a.astype(BF16), b.astype(BF16))


def _sigmoid(x):
    return 1.0 / (1.0 + jnp.exp(-x))


def _silu(x):
    return x * _sigmoid(x)


def _softplus(x):
    return jnp.maximum(x, 0.0) + jnp.log(1.0 + jnp.exp(-jnp.abs(x)))


def _resident(shape):
    nd = len(shape)
    return pl.BlockSpec(shape, lambda *_: (0,) * nd, pipeline_mode=pl.Buffered(1))


def _prep_kernel(x_ref, w_in_ref, conv_w_ref, alog_ref, dtb_ref, qng_ref, kvng_ref, w_uq_ref,
                 w_ukv_ref, cos_ref, sin_ref,
                 qg_ref, kg_ref, vg_ref, zg_ref, gbcol_ref, gbrow_ref, qa_ref, ka_ref, vt_ref,
                 cbuf, *, tm, att_scale):
    @pl.when(pl.program_id(1) == 0)
    def _():
        cbuf[0:8, :] = jnp.zeros((8, QKV_W), F32)

    xb = x_ref[0].astype(BF16)
    lane = lax.broadcasted_iota(jnp.int32, (tm, LANES), 1)
    cos = cos_ref[0]
    sin = sin_ref[0]
    first_half = (lane % MLA_ROPE) < (MLA_ROPE // 2)
    nh = MLA_HEADS * MLA_NOPE

    def proj(c0, c1):
        return _dot(xb, w_in_ref[:, c0:c1])

    def rope(blk):
        partner = jnp.where(first_half, pltpu.roll(blk, LANES - MLA_ROPE // 2, 1),
                            pltpu.roll(blk, MLA_ROPE // 2, 1))
        return blk * cos + partner * sin

    cq = proj(CQ_OFF, CKV_OFF)
    cqn = cq * lax.rsqrt(jnp.mean(cq * cq, -1, keepdims=True) + RMS_EPS) * qng_ref[...]
    qm = _dot(cqn.astype(BF16), w_uq_ref[...])
    for hd in range(MLA_HEADS):
        lo = hd * LANES
        qa_ref[0, hd, :, 0:LANES] = (qm[:, lo:lo + LANES] * att_scale).astype(BF16)
        qa_ref[0, hd, :, LANES:ATT_DK] = (rope(qm[:, nh + lo:nh + lo + LANES]) * att_scale).astype(BF16)

    tail = proj(CKV_OFF, IN_PAD)
    ckv = tail[:, :KV_LORA]
    last = tail[:, KV_LORA:]
    ckvn = ckv * lax.rsqrt(jnp.mean(ckv * ckv, -1, keepdims=True) + RMS_EPS) * kvng_ref[...]
    kvm = _dot(ckvn.astype(BF16), w_ukv_ref[...])
    k_rope = jnp.where(lane < MLA_ROPE, rope(last), 0.0).astype(BF16)
    for hd in range(MLA_HEADS):
        lo = hd * LANES
        ka_ref[0, hd, :, 0:LANES] = kvm[:, lo:lo + LANES].astype(BF16)
        ka_ref[0, hd, :, LANES:ATT_DK] = k_rope
        vt_ref[0, hd, 0] = kvm[:, nh + lo:nh + lo + LANES].T.astype(BF16)

    beta = _sigmoid(last)
    g = -jnp.exp(alog_ref[...]) * _softplus(last + dtb_ref[...])
    gb = jnp.where((lane >= BETA_LANE) & (lane < G_LANE), beta,
                   jnp.where((lane >= G_LANE) & (lane < G_LANE + GDN_HEADS), g, 0.0))
    gbcol_ref[0] = gb
    gbrow_ref[0] = gb.T[BETA_LANE:BETA_LANE + 2 * GDN_HEADS, :]

    zg_ref[0] = proj(Z_OFF, Z_OFF + GDN_VW).astype(BF16)

    for grp, out_ref in enumerate((qg_ref, kg_ref, vg_ref)):
        c0, c1 = grp * GDN_QK, (grp + 1) * GDN_QK
        pre = proj(c0, c1)
        cbuf[8:tm + 8, c0:c1] = pre
        cw = conv_w_ref[:, c0:c1]
        y = (cbuf[5:tm + 5, c0:c1] * cw[0:1, :] + cbuf[6:tm + 6, c0:c1] * cw[1:2, :]
             + cbuf[7:tm + 7, c0:c1] * cw[2:3, :] + pre * cw[3:4, :])
        cbuf[0:8, c0:c1] = cbuf[tm:tm + 8, c0:c1]
        y = _silu(y)
        if out_ref is vg_ref:
            out_ref[0] = y.astype(BF16)
            continue
        out_scale = GDN_DK ** -0.5 if out_ref is qg_ref else 1.0
        for hd in range(GDN_HEADS):
            lo = hd * GDN_DK
            yh = y[:, lo:lo + GDN_DK]
            yn = yh * (lax.rsqrt(jnp.sum(yh * yh, -1, keepdims=True) + RMS_EPS) * out_scale)
            out_ref[0, :, lo:lo + GDN_DK] = yn.astype(BF16)


def _prep_call(x, w_in_r, conv_w, alog_pad, dtb_pad, qn_g, kvn_g, w_uq_r, w_ukv_r, cos_t, sin_t,
               *, tm):
    B, S, D = x.shape
    nt = S // tm
    row = lambda w: pl.BlockSpec((1, tm, w), lambda b, i: (b, i, 0))
    head = pl.BlockSpec((1, MLA_HEADS, tm, ATT_DK), lambda b, i: (b, 0, i, 0))
    out_shape = (
        jax.ShapeDtypeStruct((B, S, GDN_QK), BF16), jax.ShapeDtypeStruct((B, S, GDN_QK), BF16),
        jax.ShapeDtypeStruct((B, S, GDN_VW), BF16), jax.ShapeDtypeStruct((B, S, GDN_VW), BF16),
        jax.ShapeDtypeStruct((B, S, LANES), F32), jax.ShapeDtypeStruct((B, 2 * GDN_HEADS, S), F32),
        jax.ShapeDtypeStruct((B, MLA_HEADS, S, ATT_DK), BF16),
        jax.ShapeDtypeStruct((B, MLA_HEADS, S, ATT_DK), BF16),
        jax.ShapeDtypeStruct((B, MLA_HEADS, nt, MLA_V, tm), BF16),
    )
    out_specs = (
        row(GDN_QK), row(GDN_QK), row(GDN_VW), row(GDN_VW), row(LANES),
        pl.BlockSpec((1, 2 * GDN_HEADS, tm), lambda b, i: (b, 0, i)),
        head, head,
        pl.BlockSpec((1, MLA_HEADS, 1, MLA_V, tm), lambda b, i: (b, 0, i, 0, 0)),
    )
    in_specs = [
        row(D), _resident(w_in_r.shape), _resident(conv_w.shape), _resident(alog_pad.shape),
        _resident(dtb_pad.shape), _resident(qn_g.shape), _resident(kvn_g.shape),
        _resident(w_uq_r.shape), _resident(w_ukv_r.shape), row(LANES), row(LANES),
    ]
    kern = functools.partial(_prep_kernel, tm=tm,
                             att_scale=(MLA_NOPE + MLA_ROPE) ** -0.5 * float(np.log2(np.e)))
    return pl.pallas_call(
        kern, out_shape=out_shape, grid=(B, nt), in_specs=in_specs, out_specs=out_specs,
        scratch_shapes=[pltpu.VMEM((tm + 8, QKV_W), F32)],
        compiler_params=pltpu.CompilerParams(dimension_semantics=("arbitrary", "arbitrary"),
                                             vmem_limit_bytes=V7X_VMEM_LIMIT),
        name="prep",
    )(x, w_in_r, conv_w, alog_pad, dtb_pad, qn_g, kvn_g, w_uq_r, w_ukv_r, cos_t, sin_t)


def _gdn_kernel(q_ref, k_ref, v_ref, z_ref, gbcol_ref, gbrow_ref, ng_ref, o_ref, state, *, nchunk):
    @pl.when(pl.program_id(1) == 0)
    def _():
        state[...] = jnp.zeros_like(state)

    C = CHUNK
    rows = nchunk * C
    r = lax.broadcasted_iota(jnp.int32, (C, C), 0)
    c = lax.broadcasted_iota(jnp.int32, (C, C), 1)
    tri_incl = r >= c
    tri_strict = r > c
    eye = (r == c).astype(F32)
    rr = lax.broadcasted_iota(jnp.int32, (rows, rows), 0)
    cc = lax.broadcasted_iota(jnp.int32, (rows, rows), 1)
    same_chunk = (rr // C) == (cc // C)
    gb_all = gbcol_ref[0]
    gcum_c = _dot((same_chunk & (rr >= cc)).astype(F32), gb_all, HIGHEST)
    gcum_r = _dot(gbrow_ref[0], (same_chunk & (rr <= cc)).astype(F32), HIGHEST)
    ng = ng_ref[...]

    items = [(ci, hd) for ci in range(nchunk) for hd in range(GDN_HEADS)]
    pre = {}
    for ci, hd in items:
        r0, lo = ci * C, hd * GDN_DK
        gc = gcum_c[r0:r0 + C, G_LANE + hd:G_LANE + hd + 1]
        gr = gcum_r[GDN_HEADS + hd:GDN_HEADS + hd + 1, r0:r0 + C]
        gl = gc[C - 1:C, :]
        beta = gb_all[r0:r0 + C, BETA_LANE + hd:BETA_LANE + hd + 1]
        decay = jnp.exp(jnp.where(tri_incl, gc - gr, -jnp.inf))
        eg = jnp.exp(gc)
        qh = q_ref[0, r0:r0 + C, lo:lo + GDN_DK].astype(F32)
        kh = k_ref[0, r0:r0 + C, lo:lo + GDN_DK].astype(F32)
        vh = v_ref[0, r0:r0 + C, lo:lo + GDN_DV].astype(F32)
        kb = kh * beta
        kq = _bdot_nt(jnp.concatenate([kb, qh], 0), kh)
        pre[ci, hd] = dict(
            low=jnp.where(tri_strict, kq[:C] * decay, 0.0),
            attn=jnp.where(tri_incl, kq[C:] * decay, 0.0),
            rhs=jnp.concatenate([vh * beta, kb * eg], 1),
            qdec=qh * eg, kdec=kh * jnp.exp(gl - gc), egl=jnp.exp(gl))

    inv = {it: eye - pre[it]["low"] for it in items}
    pw = {it: _bdot(pre[it]["low"], pre[it]["low"]) for it in items}
    span = 2
    while span < C:
        last = 2 * span >= C
        for it in items:
            if last:
                inv[it] = inv[it] + _bdot(inv[it], pw[it])
            else:
                both = _bdot(jnp.concatenate([inv[it], pw[it]], 0), pw[it])
                inv[it] = inv[it] + both[:C]
                pw[it] = both[C:]
        span *= 2
    sol = {it: _bdot(inv[it], pre[it]["rhs"]) for it in items}

    s = [state[hd] for hd in range(GDN_HEADS)]
    heads = range(GDN_HEADS)
    for ci in range(nchunk):
        r0 = ci * C
        ws = [_bdot(jnp.concatenate([sol[ci, hd][:, GDN_DV:], pre[ci, hd]["qdec"]], 0), s[hd])
              for hd in heads]
        v_new = [sol[ci, hd][:, :GDN_DV] - ws[hd][:C] for hd in heads]
        o = [ws[hd][C:] + _bdot(pre[ci, hd]["attn"], v_new[hd]) for hd in heads]
        s = [s[hd] * pre[ci, hd]["egl"] + _bdot_tn(pre[ci, hd]["kdec"], v_new[hd]) for hd in heads]
        for hd in heads:
            lo = hd * GDN_DV
            zh = z_ref[0, r0:r0 + C, lo:lo + GDN_DV].astype(F32)
            on = o[hd] * lax.rsqrt(jnp.mean(o[hd] * o[hd], -1, keepdims=True) + RMS_EPS) * ng
            o_ref[0, r0:r0 + C, lo:lo + GDN_DV] = (on * _silu(zh)).astype(BF16)
    for hd in heads:
        state[hd] = s[hd]


def _gdn_call(qg, kg, vg, zg, gbcol, gbrow, norm_g, *, rows):
    B, S, _ = qg.shape
    row = lambda w: pl.BlockSpec((1, rows, w), lambda b, i: (b, i, 0))
    kern = functools.partial(_gdn_kernel, nchunk=rows // CHUNK)
    return pl.pallas_call(
        kern, out_shape=jax.ShapeDtypeStruct((B, S, GDN_VW), BF16), grid=(B, S // rows),
        in_specs=[row(GDN_QK), row(GDN_QK), row(GDN_VW), row(GDN_VW), row(LANES),
                  pl.BlockSpec((1, 2 * GDN_HEADS, rows), lambda b, i: (b, 0, i)),
                  _resident(norm_g.shape)],
        out_specs=row(GDN_VW),
        scratch_shapes=[pltpu.VMEM((GDN_HEADS, GDN_DK, GDN_DV), F32)],
        compiler_params=pltpu.CompilerParams(dimension_semantics=("arbitrary", "arbitrary"),
                                             vmem_limit_bytes=V7X_VMEM_LIMIT),
        name="gdn",
    )(qg, kg, vg, zg, gbcol, gbrow, norm_g)


def _attn_kernel(q_ref, k_ref, vt_ref, o_ref, m_ref, l_ref, acc_ref, *, blk, hps):
    i = pl.program_id(2)
    m_ref[...] = jnp.full_like(m_ref, -jnp.inf)
    l_ref[...] = jnp.zeros_like(l_ref)
    acc_ref[...] = jnp.zeros_like(acc_ref)

    def step(j, masked):
        start = pl.multiple_of(j * blk, blk)
        st = [_dot_nt(k_ref[0, hd, pl.ds(start, blk), :], q_ref[0, hd]) for hd in range(hps)]
        if masked:
            kc = lax.broadcasted_iota(jnp.int32, (blk, blk), 0) // CHUNK
            qc = lax.broadcasted_iota(jnp.int32, (blk, blk), 1) // CHUNK
            st = [jnp.where(kc <= qc, s, -jnp.inf) for s in st]
        for hd in range(hps):
            m_old = m_ref[hd]
            m_new = jnp.maximum(m_old, jnp.max(st[hd], axis=0, keepdims=True))
            alpha = jnp.exp2(m_old - m_new)
            p = jnp.exp2(st[hd] - m_new)
            l_ref[hd] = alpha * l_ref[hd] + jnp.sum(p, axis=0, keepdims=True)
            acc_ref[hd] = alpha * acc_ref[hd] + _dot(vt_ref[0, hd, j], p.astype(BF16))
            m_ref[hd] = m_new

    step(i, True)
    lax.fori_loop(0, i, lambda j, carry: (step(j, False), carry)[1], 0)
    for hd in range(hps):
        o_ref[0, :, hd * MLA_V:(hd + 1) * MLA_V] = (acc_ref[hd] / l_ref[hd]).T.astype(BF16)


def _attn_call(qa, ka, vt, *, blk, hps):
    B, H, S, _ = qa.shape
    nb = S // blk
    kern = functools.partial(_attn_kernel, blk=blk, hps=hps)
    return pl.pallas_call(
        kern, out_shape=jax.ShapeDtypeStruct((B, S, H * MLA_V), BF16), grid=(B, H // hps, nb),
        in_specs=[pl.BlockSpec((1, hps, blk, ATT_DK), lambda b, h, i: (b, h, i, 0)),
                  pl.BlockSpec((1, hps, S, ATT_DK), lambda b, h, i: (b, h, 0, 0),
                               pipeline_mode=pl.Buffered(1)),
                  pl.BlockSpec((1, hps, nb, MLA_V, blk), lambda b, h, i: (b, h, 0, 0, 0),
                               pipeline_mode=pl.Buffered(1))],
        out_specs=pl.BlockSpec((1, blk, hps * MLA_V), lambda b, h, i: (b, i, h)),
        scratch_shapes=[pltpu.VMEM((hps, 1, blk), F32), pltpu.VMEM((hps, 1, blk), F32),
                        pltpu.VMEM((hps, MLA_V, blk), F32)],
        compiler_params=pltpu.CompilerParams(
            dimension_semantics=("arbitrary", "arbitrary", "arbitrary"),
            vmem_limit_bytes=V7X_VMEM_LIMIT),
        name="attn",
    )(qa, ka, vt)


def _layer_norm(x, g, b):
    mu = jnp.mean(x, -1, keepdims=True)
    xc = x - mu
    var = jnp.mean(xc * xc, -1, keepdims=True)
    return xc * lax.rsqrt(var + LN_EPS) * g + b


def _post_kernel(x_ref, og_ref, om_ref, p_ref, wog_ref, wom_ref, l1g_ref, l1b_ref, wg_ref, wu_ref,
                 wd_ref, l2g_ref, l2b_ref, wple_ref, wpg_ref, o_ref, *, alpha, ff_chunk):
    mix = _dot(og_ref[...], wog_ref[...]) + _dot(om_ref[...], wom_ref[...])
    x1 = _layer_norm(alpha * x_ref[...] + mix, l1g_ref[...], l1b_ref[...])
    x1b = x1.astype(BF16)
    d_ff = wd_ref.shape[0]
    ff = jnp.zeros_like(x1)
    for c0 in range(0, d_ff, ff_chunk):
        gate = _dot(x1b, wg_ref[:, c0:c0 + ff_chunk])
        up = _dot(x1b, wu_ref[:, c0:c0 + ff_chunk])
        ff = ff + _dot((_silu(gate) * up).astype(BF16), wd_ref[c0:c0 + ff_chunk, :])
    x2 = _layer_norm(alpha * x1 + ff, l2g_ref[...], l2b_ref[...])
    emb = _dot(p_ref[...].astype(BF16), wple_ref[...])
    o_ref[...] = x2 + _sigmoid(_dot(x2.astype(BF16), wpg_ref[...])) * emb


def _post_call(x2d, og, om, p2d, w_out_g, w_out_m, ln1_g, ln1_b, w_gate, w_up, w_down, ln2_g, ln2_b,
               w_ple, w_pg, *, tm, alpha, ff_chunk):
    T, D = x2d.shape
    row = lambda w: pl.BlockSpec((tm, w), lambda i: (i, 0))
    weights = (w_out_g, w_out_m, ln1_g, ln1_b, w_gate, w_up, w_down, ln2_g, ln2_b, w_ple, w_pg)
    kern = functools.partial(_post_kernel, alpha=alpha, ff_chunk=ff_chunk)
    return pl.pallas_call(
        kern, out_shape=jax.ShapeDtypeStruct((T, D), F32), grid=(T // tm,),
        in_specs=[row(D), row(og.shape[1]), row(om.shape[1]), row(p2d.shape[1])]
        + [_resident(w.shape) for w in weights],
        out_specs=row(D),
        compiler_params=pltpu.CompilerParams(dimension_semantics=("arbitrary",),
                                             vmem_limit_bytes=V7X_VMEM_LIMIT),
        name="post",
    )(x2d, og, om, p2d, *weights)


def _reorder_weights(w_in, w_uq, w_ukv, a_log, dt_bias):
    depth, d, _ = w_in.shape
    q, k, v, z, b, a, cq, ckv, kr = jnp.split(
        w_in, np.cumsum((GDN_QK, GDN_QK, GDN_VW, GDN_VW, GDN_HEADS, GDN_HEADS, Q_LORA, KV_LORA))
        .tolist(), axis=-1)
    pad = jnp.zeros((depth, d, LANES - MLA_ROPE - 2 * GDN_HEADS), w_in.dtype)
    w_in_r = jnp.concatenate([q, k, v, z, cq, ckv, kr, b, a, pad], -1).astype(BF16)

    uq = w_uq.reshape(depth, Q_LORA, MLA_HEADS, MLA_NOPE + MLA_ROPE)
    uq_nope = uq[..., :MLA_NOPE].reshape(depth, Q_LORA, MLA_HEADS * MLA_NOPE)
    uq_rope = jnp.pad(uq[..., MLA_NOPE:], ((0, 0), (0, 0), (0, 0), (0, LANES - MLA_ROPE)))
    w_uq_r = jnp.concatenate([uq_nope, uq_rope.reshape(depth, Q_LORA, MLA_HEADS * LANES)],
                             -1).astype(BF16)

    ukv = w_ukv.reshape(depth, KV_LORA, MLA_HEADS, MLA_NOPE + MLA_V)
    w_ukv_r = jnp.concatenate([ukv[..., :MLA_NOPE].reshape(depth, KV_LORA, -1),
                               ukv[..., MLA_NOPE:].reshape(depth, KV_LORA, -1)], -1).astype(BF16)

    lanes_pad = ((0, 0), (0, 0), (G_LANE, LANES - G_LANE - GDN_HEADS))
    alog_pad = jnp.pad(a_log[:, None, :], lanes_pad)
    dtb_pad = jnp.pad(dt_bias[:, None, :], lanes_pad)
    return w_in_r, w_uq_r, w_ukv_r, alog_pad, dtb_pad


def _rope_tables(positions):
    inv_freq = ROPE_THETA ** (-jnp.arange(0, MLA_ROPE, 2, dtype=F32) / MLA_ROPE)
    ang = positions.astype(F32)[..., None] * inv_freq
    cos, sin = jnp.cos(ang), jnp.sin(ang)
    reps = LANES // MLA_ROPE
    return (jnp.tile(jnp.concatenate([cos, cos], -1), (1, 1, reps)),
            jnp.tile(jnp.concatenate([-sin, sin], -1), (1, 1, reps)))


def _tile_rows(s, want):
    t = min(want, s)
    while s % t:
        t //= 2
    return t


def kernel(x, p, positions, w_in, conv_w, a_log, dt_bias, gdn_norm_g, q_norm_g, w_uq, kv_norm_g,
           w_ukv, w_out, ln1_g, ln1_b, w_gate_up, w_down, ln2_g, ln2_b, w_ple, w_ple_gate):
    B, S, D = x.shape
    depth = w_in.shape[0]
    d_ff = w_down.shape[1]
    alpha = (2.0 * depth) ** 0.25
    tm = _tile_rows(S, 512)
    gdn_rows = _tile_rows(S, 256)
    ff_chunk = d_ff // 2 if (d_ff // 2) % LANES == 0 else d_ff

    w_in_r, w_uq_r, w_ukv_r, alog_pad, dtb_pad = _reorder_weights(w_in, w_uq, w_ukv, a_log, dt_bias)
    cos_t, sin_t = _rope_tables(positions)
    w_out_b = w_out.astype(BF16)
    w_gu_b = w_gate_up.astype(BF16)
    w_down_b = w_down.astype(BF16)
    w_ple_b = w_ple.astype(BF16)
    w_pg_b = w_ple_gate.astype(BF16)
    vec = lambda a: a[:, None, :]

    for i in range(depth):
        qg, kg, vg, zg, gbcol, gbrow, qa, ka, vt = _prep_call(
            x, w_in_r[i], conv_w[i], alog_pad[i], dtb_pad[i], vec(q_norm_g)[i], vec(kv_norm_g)[i],
            w_uq_r[i], w_ukv_r[i], cos_t, sin_t, tm=tm)
        og = _gdn_call(qg, kg, vg, zg, gbcol, gbrow, vec(gdn_norm_g)[i], rows=gdn_rows)
        om = _attn_call(qa, ka, vt, blk=tm, hps=MLA_HEADS)
        x = _post_call(
            x.reshape(B * S, D), og.reshape(B * S, -1), om.reshape(B * S, -1),
            p[i].reshape(B * S, -1), w_out_b[i, :GDN_VW], w_out_b[i, GDN_VW:], vec(ln1_g)[i],
            vec(ln1_b)[i], w_gu_b[i, :, :d_ff], w_gu_b[i, :, d_ff:], w_down_b[i], vec(ln2_g)[i],
            vec(ln2_b)[i], w_ple_b[i], w_pg_b[i], tm=tm, alpha=alpha, ff_chunk=ff_chunk,
        ).reshape(B, S, D)
    return x
```

```python
import functools

import jax
import jax.numpy as jnp
import numpy as np
from jax import lax
from jax.experimental import pallas as pl
from jax.experimental.pallas import tpu as pltpu

F32 = jnp.float32
BF16 = jnp.bfloat16

CHUNK = 64
GDN_HEADS = 4
GDN_DK = 128
GDN_DV = 128
CONV_WIDTH = 4
MLA_HEADS = 4
MLA_NOPE = 128
MLA_ROPE = 64
MLA_V = 128
Q_LORA = 384
KV_LORA = 256
ROPE_THETA = 10000.0
LN_EPS = 1e-5
RMS_EPS = 1e-6

GDN_QK = GDN_HEADS * GDN_DK
GDN_VW = GDN_HEADS * GDN_DV
QKV_W = 2 * GDN_QK + GDN_VW
LANES = 128
Z_OFF = QKV_W
CQ_OFF = Z_OFF + GDN_VW
CKV_OFF = CQ_OFF + Q_LORA
LAST_OFF = CKV_OFF + KV_LORA
IN_PAD = LAST_OFF + LANES
BETA_LANE = MLA_ROPE
G_LANE = MLA_ROPE + GDN_HEADS
ATT_DK = 2 * LANES
VT_ROWS = MLA_V + 16

V7X_VMEM_LIMIT = 56 * 1024 * 1024

HIGHEST = lax.Precision.HIGHEST


def _dot(a, b, precision=None):
    return jnp.dot(a, b, preferred_element_type=F32, precision=precision)


def _dot_nt(a, b, precision=None):
    return lax.dot_general(a, b, (((1,), (1,)), ((), ())), preferred_element_type=F32,
                           precision=precision)


def _dot_tn(a, b, precision=None):
    return lax.dot_general(a, b, (((0,), (0,)), ((), ())), preferred_element_type=F32,
                           precision=precision)


def _bdot(a, b):
    return _dot(a.astype(BF16), b.astype(BF16))


def _bdot_nt(a, b):
    return _dot_nt(a.astype(BF16), b.astype(BF16))


def _bdot_tn(a, b):
    return _dot_tn(a.astype(BF16), b.astype(BF16))


def _sigmoid(x):
    return 1.0 / (1.0 + jnp.exp(-x))


def _silu(x):
    return x * _sigmoid(x)


def _softplus(x):
    return jnp.maximum(x, 0.0) + jnp.log(1.0 + jnp.exp(-jnp.abs(x)))


def _layer_spec(arr, layer, split=None):
    block = list(arr.shape[1:])
    index = [0] * len(block)
    if split is not None:
        axis, parts, which = split
        block[axis] //= parts
        index[axis] = which
    return pl.BlockSpec((None, *block), lambda *_: (layer, *index), pipeline_mode=pl.Buffered(1))


def _prep_kernel(x_ref, w_in_ref, conv_w_ref, alog_ref, dtb_ref, qng_ref, kvng_ref, w_uq_ref,
                 w_ukv_ref, cos_ref, sin_ref,
                 qg_ref, kg_ref, vg_ref, zg_ref, gbcol_ref, gbrow_ref, qa_ref, ka_ref, vt_ref,
                 cbuf, *, tm, att_scale):
    @pl.when(pl.program_id(1) == 0)
    def _():
        cbuf[...] = jnp.zeros_like(cbuf)

    row8 = lax.broadcasted_iota(jnp.int32, (8, GDN_QK), 0)

    xb = x_ref[0].astype(BF16)
    lane = lax.broadcasted_iota(jnp.int32, (tm, LANES), 1)
    cos = cos_ref[0]
    sin = sin_ref[0]
    first_half = (lane % MLA_ROPE) < (MLA_ROPE // 2)
    nh = MLA_HEADS * MLA_NOPE

    def proj(c0, c1):
        return _dot(xb, w_in_ref[:, c0:c1])

    def rope(blk):
        partner = jnp.where(first_half, pltpu.roll(blk, LANES - MLA_ROPE // 2, 1),
                            pltpu.roll(blk, MLA_ROPE // 2, 1))
        return blk * cos + partner * sin

    def conv_group(grp, out_ref):
        c0, c1 = grp * GDN_QK, (grp + 1) * GDN_QK
        pre = proj(c0, c1)
        cw = conv_w_ref[:, c0:c1]
        y = pre * cw[0:1, :]
        for j in range(1, CONV_WIDTH):
            carry_in = cbuf[j - 1:j, c0:c1]
            cbuf[j - 1:j, c0:c1] = y[tm - 1:tm, :]
            shifted = pltpu.roll(y, 1, 0)
            top = jnp.where(row8 == 0, carry_in, shifted[0:8, :])
            y = jnp.concatenate([top, shifted[8:, :]], 0) + pre * cw[j:j + 1, :]
        y = _silu(y)
        if out_ref is vg_ref:
            out_ref[0] = y.astype(BF16)
            return
        out_scale = GDN_DK ** -0.5 if out_ref is qg_ref else 1.0
        for hd in range(GDN_HEADS):
            lo = hd * GDN_DK
            yh = y[:, lo:lo + GDN_DK]
            yn = yh * (lax.rsqrt(jnp.sum(yh * yh, -1, keepdims=True) + RMS_EPS) * out_scale)
            out_ref[0, :, lo:lo + GDN_DK] = yn.astype(BF16)

    for grp, out_ref in enumerate((qg_ref, kg_ref, vg_ref)):
        conv_group(grp, out_ref)

    cq = proj(CQ_OFF, CKV_OFF)
    cqn = cq * lax.rsqrt(jnp.mean(cq * cq, -1, keepdims=True) + RMS_EPS) * qng_ref[...]
    qm = _dot(cqn.astype(BF16), w_uq_ref[...])
    for hd in range(MLA_HEADS):
        lo = hd * LANES
        q_rope = qm[:, nh + lo:nh + lo + LANES] * cos + qm[:, 2 * nh + lo:2 * nh + lo + LANES] * sin
        qa_ref[0, hd] = (jnp.concatenate([qm[:, lo:lo + LANES], q_rope], -1) * att_scale).astype(BF16)

    tail = proj(CKV_OFF, IN_PAD)
    ckv = tail[:, :KV_LORA]
    last = tail[:, KV_LORA:]
    ckvn = ckv * lax.rsqrt(jnp.mean(ckv * ckv, -1, keepdims=True) + RMS_EPS) * kvng_ref[...]
    kvm = _dot(ckvn.astype(BF16), w_ukv_ref[...])
    k_rope = jnp.where(lane < MLA_ROPE, rope(last), 0.0)
    ones_rows = (lax.broadcasted_iota(jnp.int32, (VT_ROWS - MLA_V, tm), 0) == 0).astype(BF16)
    for hd in range(MLA_HEADS):
        lo = hd * LANES
        ka_ref[0, hd] = jnp.concatenate([kvm[:, lo:lo + LANES], k_rope], -1).astype(BF16)
        vt_ref[0, hd, 0, 0:MLA_V, :] = kvm[:, nh + lo:nh + lo + LANES].T.astype(BF16)
        vt_ref[0, hd, 0, MLA_V:VT_ROWS, :] = ones_rows

    beta = _sigmoid(last)
    g = -jnp.exp(alog_ref[...]) * _softplus(last + dtb_ref[...])
    gb = jnp.where((lane >= BETA_LANE) & (lane < G_LANE), beta,
                   jnp.where((lane >= G_LANE) & (lane < G_LANE + GDN_HEADS), g, 0.0))
    gbcol_ref[0] = gb
    gbrow_ref[0] = gb.T[BETA_LANE:BETA_LANE + 2 * GDN_HEADS, :]

    zg_ref[0] = proj(Z_OFF, Z_OFF + GDN_VW).astype(BF16)


def _prep_call(x, layer, w_in_r, conv_w, alog_pad, dtb_pad, qn_g, kvn_g, w_uq_r, w_ukv_r, cos_t,
               sin_t, *, tm):
    B, S, D = x.shape
    params = (w_in_r, conv_w, alog_pad, dtb_pad, qn_g, kvn_g, w_uq_r, w_ukv_r)
    nt = S // tm
    row = lambda w: pl.BlockSpec((1, tm, w), lambda b, i: (b, i, 0))
    head = pl.BlockSpec((1, MLA_HEADS, tm, ATT_DK), lambda b, i: (b, 0, i, 0))
    out_shape = (
        jax.ShapeDtypeStruct((B, S, GDN_QK), BF16), jax.ShapeDtypeStruct((B, S, GDN_QK), BF16),
        jax.ShapeDtypeStruct((B, S, GDN_VW), BF16), jax.ShapeDtypeStruct((B, S, GDN_VW), BF16),
        jax.ShapeDtypeStruct((B, S, LANES), F32), jax.ShapeDtypeStruct((B, 2 * GDN_HEADS, S), F32),
        jax.ShapeDtypeStruct((B, MLA_HEADS, S, ATT_DK), BF16),
        jax.ShapeDtypeStruct((B, MLA_HEADS, S, ATT_DK), BF16),
        jax.ShapeDtypeStruct((B, MLA_HEADS, nt, VT_ROWS, tm), BF16),
    )
    out_specs = (
        row(GDN_QK), row(GDN_QK), row(GDN_VW), row(GDN_VW), row(LANES),
        pl.BlockSpec((1, 2 * GDN_HEADS, tm), lambda b, i: (b, 0, i)),
        head, head,
        pl.BlockSpec((1, MLA_HEADS, 1, VT_ROWS, tm), lambda b, i: (b, 0, i, 0, 0)),
    )
    in_specs = [row(D)] + [_layer_spec(a, layer) for a in params] + [row(LANES), row(LANES)]
    kern = functools.partial(_prep_kernel, tm=tm,
                             att_scale=(MLA_NOPE + MLA_ROPE) ** -0.5 * float(np.log2(np.e)))
    return pl.pallas_call(
        kern, out_shape=out_shape, grid=(B, nt), in_specs=in_specs, out_specs=out_specs,
        scratch_shapes=[pltpu.VMEM((8, QKV_W), F32)],
        compiler_params=pltpu.CompilerParams(dimension_semantics=("arbitrary", "arbitrary"),
                                             vmem_limit_bytes=V7X_VMEM_LIMIT),
        name="prep",
    )(x, *params, cos_t, sin_t)


def _gdn_kernel(q_ref, k_ref, v_ref, z_ref, gbcol_ref, gbrow_ref, ng_ref, o_ref, state, *, nchunk,
                group):
    @pl.when(pl.program_id(1) == 0)
    def _():
        state[...] = jnp.zeros_like(state)

    C = CHUNK
    r = lax.broadcasted_iota(jnp.int32, (C, C), 0)
    c = lax.broadcasted_iota(jnp.int32, (C, C), 1)
    tri_incl = r >= c
    tri_strict = r > c
    eye = (r == c).astype(F32)
    grows = group * C
    rr = lax.broadcasted_iota(jnp.int32, (grows, grows), 0)
    cc = lax.broadcasted_iota(jnp.int32, (grows, grows), 1)
    same_chunk = (rr // C) == (cc // C)
    cum_down = (same_chunk & (rr >= cc)).astype(F32)
    cum_right = (same_chunk & (rr <= cc)).astype(F32)
    ng = ng_ref[...]

    heads = range(GDN_HEADS)

    def prepare(chunks, out):
        g0 = chunks[0] * C
        gb_grp = gbcol_ref[0, g0:g0 + grows, :]
        gcum_c = _dot(cum_down, gb_grp, HIGHEST)
        gcum_r = _dot(gbrow_ref[0, :, g0:g0 + grows], cum_right, HIGHEST)
        low, attn, rhs, qdec, kdec, egl = [], [], [], [], [], []
        for ci, hd in [(ci, hd) for ci in chunks for hd in heads]:
            r0, lo = ci * C, hd * GDN_DK
            gc = gcum_c[r0 - g0:r0 - g0 + C, G_LANE + hd:G_LANE + hd + 1]
            gr = gcum_r[GDN_HEADS + hd:GDN_HEADS + hd + 1, r0 - g0:r0 - g0 + C]
            gl = gc[C - 1:C, :]
            beta = gb_grp[r0 - g0:r0 - g0 + C, BETA_LANE + hd:BETA_LANE + hd + 1]
            decay = jnp.exp(jnp.where(tri_incl, gc - gr, -jnp.inf))
            eg = jnp.exp(gc)
            qh = q_ref[0, r0:r0 + C, lo:lo + GDN_DK].astype(F32)
            kh = k_ref[0, r0:r0 + C, lo:lo + GDN_DK].astype(F32)
            vh = v_ref[0, r0:r0 + C, lo:lo + GDN_DV].astype(F32)
            kb = kh * beta
            kq = _bdot_nt(jnp.concatenate([kb, qh], 0), kh)
            low.append(jnp.where(tri_strict, kq[:C] * decay, 0.0))
            attn.append(jnp.where(tri_incl, kq[C:] * decay, 0.0))
            rhs.append(jnp.concatenate([vh * beta, kb * eg], 1))
            qdec.append(qh * eg)
            kdec.append(kh * jnp.exp(gl - gc))
            egl.append(jnp.exp(gl))
        yield
        items = range(len(low))
        inv = [eye - low[it] for it in items]
        pw = [_bdot(low[it], low[it]) for it in items]
        yield
        span = 2
        while span < C:
            if 2 * span >= C:
                inv = [inv[it] + _bdot(inv[it], pw[it]) for it in items]
            else:
                both = [_bdot(jnp.concatenate([inv[it], pw[it]], 0), pw[it]) for it in items]
                inv = [inv[it] + both[it][:C] for it in items]
                pw = [both[it][C:] for it in items]
            span *= 2
            yield
        sol = [_bdot(inv[it], rhs[it]) for it in items]
        for n, ci in enumerate(chunks):
            sl = slice(n * GDN_HEADS, (n + 1) * GDN_HEADS)
            out[ci] = dict(u=[x[:, :GDN_DV] for x in sol[sl]], w=[x[:, GDN_DV:] for x in sol[sl]],
                           attn=attn[sl], qdec=qdec[sl], kdec=kdec[sl], egl=egl[sl])
        yield

    def recur_chunk(ci, pre, s):
        r0 = ci * C
        ws = [_bdot(jnp.concatenate([pre["w"][hd], pre["qdec"][hd]], 0), s[hd]) for hd in heads]
        yield
        v_new = [pre["u"][hd] - ws[hd][:C] for hd in heads]
        o = [ws[hd][C:] + _bdot(pre["attn"][hd], v_new[hd]) for hd in heads]
        for hd in heads:
            s[hd] = s[hd] * pre["egl"][hd] + _bdot_tn(pre["kdec"][hd], v_new[hd])
        yield
        for hd in heads:
            lo = hd * GDN_DV
            zh = z_ref[0, r0:r0 + C, lo:lo + GDN_DV].astype(F32)
            on = o[hd] * lax.rsqrt(jnp.mean(o[hd] * o[hd], -1, keepdims=True) + RMS_EPS) * ng
            o_ref[0, r0:r0 + C, lo:lo + GDN_DV] = (on * _silu(zh)).astype(BF16)
        yield

    def recur(chunks, prepared, s):
        for ci in chunks:
            yield from recur_chunk(ci, prepared[ci], s)

    s = [state[hd] for hd in heads]
    groups = [list(range(g, min(g + group, nchunk))) for g in range(0, nchunk, group)]
    prepared = {}
    for _ in prepare(groups[0], prepared):
        pass
    for gi, chunks in enumerate(groups):
        nxt = prepare(groups[gi + 1], prepared) if gi + 1 < len(groups) else iter(())
        cur = recur(chunks, prepared, s)
        nxt_done = cur_done = False
        while not (nxt_done and cur_done):
            nxt_done = nxt_done or next(nxt, "end") == "end"
            cur_done = cur_done or next(cur, "end") == "end"
    for hd in heads:
        state[hd] = s[hd]


def _gdn_call(qg, kg, vg, zg, gbcol, gbrow, layer, norm_g, *, rows, group):
    B, S, _ = qg.shape
    row = lambda w: pl.BlockSpec((1, rows, w), lambda b, i: (b, i, 0))
    kern = functools.partial(_gdn_kernel, nchunk=rows // CHUNK, group=group)
    return pl.pallas_call(
        kern, out_shape=jax.ShapeDtypeStruct((B, S, GDN_VW), BF16), grid=(B, S // rows),
        in_specs=[row(GDN_QK), row(GDN_QK), row(GDN_VW), row(GDN_VW), row(LANES),
                  pl.BlockSpec((1, 2 * GDN_HEADS, rows), lambda b, i: (b, 0, i)),
                  _layer_spec(norm_g, layer)],
        out_specs=row(GDN_VW),
        scratch_shapes=[pltpu.VMEM((GDN_HEADS, GDN_DK, GDN_DV), F32)],
        compiler_params=pltpu.CompilerParams(dimension_semantics=("arbitrary", "arbitrary"),
                                             vmem_limit_bytes=V7X_VMEM_LIMIT),
        name="gdn",
    )(qg, kg, vg, zg, gbcol, gbrow, norm_g)


def _attn_kernel(q_ref, k_ref, vt_ref, o_ref, m_ref, acc_ref, *, blk, hps):
    i = pl.program_id(2)
    m_ref[...] = jnp.full_like(m_ref, -jnp.inf)
    acc_ref[...] = jnp.zeros_like(acc_ref)

    def step(j, masked):
        start = pl.multiple_of(j * blk, blk)
        st = [_dot_nt(k_ref[0, hd, pl.ds(start, blk), :], q_ref[0, hd]) for hd in range(hps)]
        if masked:
            kc = lax.broadcasted_iota(jnp.int32, (blk, blk), 0) // CHUNK
            qc = lax.broadcasted_iota(jnp.int32, (blk, blk), 1) // CHUNK
            st = [jnp.where(kc <= qc, s, -jnp.inf) for s in st]
        for hd in range(hps):
            m_old = m_ref[hd]
            m_new = jnp.maximum(m_old, jnp.max(st[hd], axis=0, keepdims=True))
            alpha = jnp.exp2(m_old - m_new)
            p = jnp.exp2(st[hd] - m_new)
            acc_ref[hd] = alpha * acc_ref[hd] + _dot(vt_ref[0, hd, j], p.astype(BF16))
            m_ref[hd] = m_new

    step(i, True)
    lax.fori_loop(0, i, lambda j, carry: (step(j, False), carry)[1], 0)
    for hd in range(hps):
        out = acc_ref[hd, 0:MLA_V, :] / acc_ref[hd, MLA_V:MLA_V + 1, :]
        o_ref[0, :, hd * MLA_V:(hd + 1) * MLA_V] = out.T.astype(BF16)


def _attn_call(qa, ka, vt, *, blk, hps):
    B, H, S, _ = qa.shape
    nb = S // blk
    kern = functools.partial(_attn_kernel, blk=blk, hps=hps)
    return pl.pallas_call(
        kern, out_shape=jax.ShapeDtypeStruct((B, S, H * MLA_V), BF16), grid=(B, H // hps, nb),
        in_specs=[pl.BlockSpec((1, hps, blk, ATT_DK), lambda b, h, i: (b, h, i, 0)),
                  pl.BlockSpec((1, hps, S, ATT_DK), lambda b, h, i: (b, h, 0, 0),
                               pipeline_mode=pl.Buffered(1)),
                  pl.BlockSpec((1, hps, nb, VT_ROWS, blk), lambda b, h, i: (b, h, 0, 0, 0),
                               pipeline_mode=pl.Buffered(1))],
        out_specs=pl.BlockSpec((1, blk, hps * MLA_V), lambda b, h, i: (b, i, h)),
        scratch_shapes=[pltpu.VMEM((hps, 1, blk), F32), pltpu.VMEM((hps, VT_ROWS, blk), F32)],
        compiler_params=pltpu.CompilerParams(
            dimension_semantics=("arbitrary", "arbitrary", "arbitrary"),
            vmem_limit_bytes=V7X_VMEM_LIMIT),
        name="attn",
    )(qa, ka, vt)


def _layer_norm(x, g, b):
    mu = jnp.mean(x, -1, keepdims=True)
    xc = x - mu
    var = jnp.mean(xc * xc, -1, keepdims=True)
    return xc * lax.rsqrt(var + LN_EPS) * g + b


def _post_kernel(x_ref, og_ref, om_ref, p_ref, wog_ref, wom_ref, l1g_ref, l1b_ref, wg_ref, wu_ref,
                 wd_ref, l2g_ref, l2b_ref, wple_ref, wpg_ref, o_ref, *, alpha, ff_chunk):
    mix = _dot(og_ref[...], wog_ref[...]) + _dot(om_ref[...], wom_ref[...])
    x1 = _layer_norm(alpha * x_ref[...] + mix, l1g_ref[...], l1b_ref[...])
    x1b = x1.astype(BF16)
    d_ff = wd_ref.shape[0]
    ff = jnp.zeros_like(x1)
    for c0 in range(0, d_ff, ff_chunk):
        gate = _dot(x1b, wg_ref[:, c0:c0 + ff_chunk])
        up = _dot(x1b, wu_ref[:, c0:c0 + ff_chunk])
        ff = ff + _dot((_silu(gate) * up).astype(BF16), wd_ref[c0:c0 + ff_chunk, :])
    x2 = _layer_norm(alpha * x1 + ff, l2g_ref[...], l2b_ref[...])
    emb = _dot(p_ref[...].astype(BF16), wple_ref[...])
    o_ref[...] = x2 + _sigmoid(_dot(x2.astype(BF16), wpg_ref[...])) * emb


def _post_call(x2d, og, om, p3d, layer, w_out, ln1_g, ln1_b, w_gate_up, w_down, ln2_g, ln2_b, w_ple,
               w_pg, *, tm, alpha, ff_chunk):
    T, D = x2d.shape
    row = lambda w: pl.BlockSpec((tm, w), lambda i: (i, 0))
    weights = ((w_out, (0, 2, 0)), (w_out, (0, 2, 1)), (ln1_g, None), (ln1_b, None),
               (w_gate_up, (1, 2, 0)), (w_gate_up, (1, 2, 1)), (w_down, None), (ln2_g, None),
               (ln2_b, None), (w_ple, None), (w_pg, None))
    kern = functools.partial(_post_kernel, alpha=alpha, ff_chunk=ff_chunk)
    return pl.pallas_call(
        kern, out_shape=jax.ShapeDtypeStruct((T, D), F32), grid=(T // tm,),
        in_specs=[row(D), row(og.shape[1]), row(om.shape[1]),
                  pl.BlockSpec((None, tm, p3d.shape[2]), lambda i: (layer, i, 0))]
        + [_layer_spec(w, layer, split) for w, split in weights],
        out_specs=row(D),
        compiler_params=pltpu.CompilerParams(dimension_semantics=("arbitrary",),
                                             vmem_limit_bytes=V7X_VMEM_LIMIT),
        name="post",
    )(x2d, og, om, p3d, *[w for w, _ in weights])


def _reorder_weights(w_in, w_uq, w_ukv, a_log, dt_bias):
    depth, d, _ = w_in.shape
    q, k, v, z, b, a, cq, ckv, kr = jnp.split(
        w_in, np.cumsum((GDN_QK, GDN_QK, GDN_VW, GDN_VW, GDN_HEADS, GDN_HEADS, Q_LORA, KV_LORA))
        .tolist(), axis=-1)
    pad = jnp.zeros((depth, d, LANES - MLA_ROPE - 2 * GDN_HEADS), w_in.dtype)
    w_in_r = jnp.concatenate([q, k, v, z, cq, ckv, kr, b, a, pad], -1).astype(BF16)

    uq = w_uq.reshape(depth, Q_LORA, MLA_HEADS, MLA_NOPE + MLA_ROPE)
    uq_nope = uq[..., :MLA_NOPE].reshape(depth, Q_LORA, MLA_HEADS * MLA_NOPE)
    lane_pad = ((0, 0), (0, 0), (0, 0), (0, LANES - MLA_ROPE))
    rope_cols = uq[..., MLA_NOPE:]
    partner_cols = jnp.concatenate([rope_cols[..., MLA_ROPE // 2:], rope_cols[..., :MLA_ROPE // 2]], -1)
    w_uq_r = jnp.concatenate(
        [uq_nope, jnp.pad(rope_cols, lane_pad).reshape(depth, Q_LORA, MLA_HEADS * LANES),
         jnp.pad(partner_cols, lane_pad).reshape(depth, Q_LORA, MLA_HEADS * LANES)], -1).astype(BF16)

    ukv = w_ukv.reshape(depth, KV_LORA, MLA_HEADS, MLA_NOPE + MLA_V)
    w_ukv_r = jnp.concatenate([ukv[..., :MLA_NOPE].reshape(depth, KV_LORA, -1),
                               ukv[..., MLA_NOPE:].reshape(depth, KV_LORA, -1)], -1).astype(BF16)

    lanes_pad = ((0, 0), (0, 0), (G_LANE, LANES - G_LANE - GDN_HEADS))
    alog_pad = jnp.pad(a_log[:, None, :], lanes_pad)
    dtb_pad = jnp.pad(dt_bias[:, None, :], lanes_pad)
    return w_in_r, w_uq_r, w_ukv_r, alog_pad, dtb_pad


def _rope_tables(positions):
    inv_freq = ROPE_THETA ** (-jnp.arange(0, MLA_ROPE, 2, dtype=F32) / MLA_ROPE)
    ang = positions.astype(F32)[..., None] * inv_freq
    cos, sin = jnp.cos(ang), jnp.sin(ang)
    reps = LANES // MLA_ROPE
    return (jnp.tile(jnp.concatenate([cos, cos], -1), (1, 1, reps)),
            jnp.tile(jnp.concatenate([-sin, sin], -1), (1, 1, reps)))


def _tile_rows(s, want):
    t = min(want, s)
    while s % t:
        t //= 2
    return t


def kernel(x, p, positions, w_in, conv_w, a_log, dt_bias, gdn_norm_g, q_norm_g, w_uq, kv_norm_g,
           w_ukv, w_out, ln1_g, ln1_b, w_gate_up, w_down, ln2_g, ln2_b, w_ple, w_ple_gate):
    B, S, D = x.shape
    depth = w_in.shape[0]
    d_ff = w_down.shape[1]
    alpha = (2.0 * depth) ** 0.25
    tm = _tile_rows(S, 512)
    gdn_rows = _tile_rows(S, 1024)
    ff_chunk = d_ff // 2 if (d_ff // 2) % LANES == 0 else d_ff

    w_in_r, w_uq_r, w_ukv_r, alog_pad, dtb_pad = _reorder_weights(w_in, w_uq, w_ukv, a_log, dt_bias)
    cos_t, sin_t = _rope_tables(positions)
    w_out_b = w_out.astype(BF16)
    w_gu_b = w_gate_up.astype(BF16)
    w_down_b = w_down.astype(BF16)
    w_ple_b = w_ple.astype(BF16)
    w_pg_b = w_ple_gate.astype(BF16)
    vec = lambda a: a[:, None, :]
    p3d = p.reshape(depth, B * S, -1)

    for i in range(depth):
        qg, kg, vg, zg, gbcol, gbrow, qa, ka, vt = _prep_call(
            x, i, w_in_r, conv_w, alog_pad, dtb_pad, vec(q_norm_g), vec(kv_norm_g), w_uq_r, w_ukv_r,
            cos_t, sin_t, tm=tm)
        og = _gdn_call(qg, kg, vg, zg, gbcol, gbrow, i, vec(gdn_norm_g), rows=gdn_rows, group=4)
        om = _attn_call(qa, ka, vt, blk=tm, hps=MLA_HEADS)
        x = _post_call(
            x.reshape(B * S, D), og.reshape(B * S, -1), om.reshape(B * S, -1), p3d, i, w_out_b,
            vec(ln1_g), vec(ln1_b), w_gu_b, w_down_b, vec(ln2_g), vec(ln2_b), w_ple_b, w_pg_b,
            tm=tm, alpha=alpha, ff_chunk=ff_chunk,
        ).reshape(B, S, D)
    return x
```

```python
import functools

import jax
import jax.numpy as jnp
import numpy as np
from jax import lax
from jax.experimental import pallas as pl
from jax.experimental.pallas import tpu as pltpu

F32 = jnp.float32
BF16 = jnp.bfloat16

CHUNK = 64
GDN_HEADS = 4
GDN_DK = 128
GDN_DV = 128
CONV_WIDTH = 4
MLA_HEADS = 4
MLA_NOPE = 128
MLA_ROPE = 64
MLA_V = 128
Q_LORA = 384
KV_LORA = 256
ROPE_THETA = 10000.0
LN_EPS = 1e-5
RMS_EPS = 1e-6

GDN_QK = GDN_HEADS * GDN_DK
GDN_VW = GDN_HEADS * GDN_DV
QKV_W = 2 * GDN_QK + GDN_VW
LANES = 128
Z_OFF = QKV_W
CQ_OFF = Z_OFF + GDN_VW
CKV_OFF = CQ_OFF + Q_LORA
LAST_OFF = CKV_OFF + KV_LORA
IN_PAD = LAST_OFF + LANES
BETA_LANE = MLA_ROPE
G_LANE = MLA_ROPE + GDN_HEADS
ATT_DK = 2 * LANES
VT_ROWS = MLA_V + 16

V7X_VMEM_LIMIT = 56 * 1024 * 1024
V7X_MXU_COLS = 256

HIGHEST = lax.Precision.HIGHEST


def _dot(a, b, precision=None):
    return jnp.dot(a, b, preferred_element_type=F32, precision=precision)


def _dot_nt(a, b, precision=None):
    return lax.dot_general(a, b, (((1,), (1,)), ((), ())), preferred_element_type=F32,
                           precision=precision)


def _dot_tn(a, b, precision=None):
    return lax.dot_general(a, b, (((0,), (0,)), ((), ())), preferred_element_type=F32,
                           precision=precision)


def _bdot(a, b):
    return _dot(a.astype(BF16), b.astype(BF16))


def _bdot_nt(a, b):
    return _dot_nt(a.astype(BF16), b.astype(BF16))


def _bdot_tn(a, b):
    return _dot_tn(a.astype(BF16), b.astype(BF16))


def _sigmoid(x):
    return 1.0 / (1.0 + jnp.exp(-x))


def _silu(x):
    return x * _sigmoid(x)


def _softplus(x):
    return jnp.maximum(x, 0.0) + jnp.log(1.0 + jnp.exp(-jnp.abs(x)))


def _layer_spec(arr, layer, split=None):
    block = list(arr.shape[1:])
    index = [0] * len(block)
    if split is not None:
        axis, parts, which = split
        block[axis] //= parts
        index[axis] = which
    return pl.BlockSpec((None, *block), lambda *_: (layer, *index), pipeline_mode=pl.Buffered(1))


def _prep_kernel(x_ref, w_in_ref, conv_w_ref, alog_ref, dtb_ref, qng_ref, kvng_ref, w_uq_ref,
                 w_ukv_ref, cos_ref, sin_ref,
                 qg_ref, kg_ref, vg_ref, zg_ref, gbcol_ref, gbrow_ref, qt_ref, ka_ref, vt_ref,
                 cbuf, *, tm, att_scale):
    @pl.when(pl.program_id(1) == 0)
    def _():
        cbuf[...] = jnp.zeros_like(cbuf)

    row8 = lax.broadcasted_iota(jnp.int32, (8, GDN_QK), 0)

    xb = x_ref[0].astype(BF16)
    lane = lax.broadcasted_iota(jnp.int32, (tm, LANES), 1)
    cos = cos_ref[0]
    sin = sin_ref[0]
    first_half = (lane % MLA_ROPE) < (MLA_ROPE // 2)
    nh = MLA_HEADS * MLA_NOPE

    def proj(c0, c1):
        return _dot(xb, w_in_ref[:, c0:c1])

    def rope(blk):
        partner = jnp.where(first_half, pltpu.roll(blk, LANES - MLA_ROPE // 2, 1),
                            pltpu.roll(blk, MLA_ROPE // 2, 1))
        return blk * cos + partner * sin

    def conv_group(grp, out_ref):
        c0, c1 = grp * GDN_QK, (grp + 1) * GDN_QK
        pre = proj(c0, c1)
        cw = conv_w_ref[:, c0:c1]
        y = pre * cw[0:1, :]
        for j in range(1, CONV_WIDTH):
            carry_in = cbuf[j - 1:j, c0:c1]
            cbuf[j - 1:j, c0:c1] = y[tm - 1:tm, :]
            shifted = pltpu.roll(y, 1, 0)
            top = jnp.where(row8 == 0, carry_in, shifted[0:8, :])
            y = jnp.concatenate([top, shifted[8:, :]], 0) + pre * cw[j:j + 1, :]
        y = _silu(y)
        if out_ref is vg_ref:
            out_ref[0] = y.astype(BF16)
            return
        out_scale = GDN_DK ** -0.5 if out_ref is qg_ref else 1.0
        for hd in range(GDN_HEADS):
            lo = hd * GDN_DK
            yh = y[:, lo:lo + GDN_DK]
            yn = yh * (lax.rsqrt(jnp.sum(yh * yh, -1, keepdims=True) + RMS_EPS) * out_scale)
            out_ref[0, :, lo:lo + GDN_DK] = yn.astype(BF16)

    for grp, out_ref in enumerate((qg_ref, kg_ref, vg_ref)):
        conv_group(grp, out_ref)

    latent = proj(CQ_OFF, IN_PAD)
    cq = latent[:, :Q_LORA]
    cqn = cq * lax.rsqrt(jnp.mean(cq * cq, -1, keepdims=True) + RMS_EPS) * qng_ref[...]
    qm = _dot(cqn.astype(BF16), w_uq_ref[...])
    for hd in range(MLA_HEADS):
        lo = hd * LANES
        q_rope = qm[:, nh + lo:nh + lo + LANES] * cos + qm[:, 2 * nh + lo:2 * nh + lo + LANES] * sin
        q_full = jnp.concatenate([qm[:, lo:lo + LANES], q_rope], -1) * att_scale
        qt_ref[0, hd, 0] = q_full.T.astype(BF16)

    ckv = latent[:, Q_LORA:Q_LORA + KV_LORA]
    last = latent[:, Q_LORA + KV_LORA:]
    ckvn = ckv * lax.rsqrt(jnp.mean(ckv * ckv, -1, keepdims=True) + RMS_EPS) * kvng_ref[...]
    kvm = _dot(ckvn.astype(BF16), w_ukv_ref[...])
    k_rope = jnp.where(lane < MLA_ROPE, rope(last), 0.0)
    ones_rows = (lax.broadcasted_iota(jnp.int32, (VT_ROWS - MLA_V, tm), 0) == 0).astype(BF16)
    for hd in range(MLA_HEADS):
        lo = hd * LANES
        ka_ref[0, hd] = jnp.concatenate([kvm[:, lo:lo + LANES], k_rope], -1).astype(BF16)
        vt_ref[0, hd, 0, 0:MLA_V, :] = kvm[:, nh + lo:nh + lo + LANES].T.astype(BF16)
        vt_ref[0, hd, 0, MLA_V:VT_ROWS, :] = ones_rows

    beta = _sigmoid(last)
    g = -jnp.exp(alog_ref[...]) * _softplus(last + dtb_ref[...])
    gb = jnp.where((lane >= BETA_LANE) & (lane < G_LANE), beta,
                   jnp.where((lane >= G_LANE) & (lane < G_LANE + GDN_HEADS), g, 0.0))
    gbcol_ref[0] = gb
    gbrow_ref[0] = gb.T[BETA_LANE:BETA_LANE + 2 * GDN_HEADS, :]

    zg_ref[0] = proj(Z_OFF, Z_OFF + GDN_VW).astype(BF16)


def _prep_call(x, layer, w_in_r, conv_w, alog_pad, dtb_pad, qn_g, kvn_g, w_uq_r, w_ukv_r, cos_t,
               sin_t, *, tm):
    B, S, D = x.shape
    params = (w_in_r, conv_w, alog_pad, dtb_pad, qn_g, kvn_g, w_uq_r, w_ukv_r)
    nt = S // tm
    row = lambda w: pl.BlockSpec((1, tm, w), lambda b, i: (b, i, 0))
    head = pl.BlockSpec((1, MLA_HEADS, tm, ATT_DK), lambda b, i: (b, 0, i, 0))
    out_shape = (
        jax.ShapeDtypeStruct((B, S, GDN_QK), BF16), jax.ShapeDtypeStruct((B, S, GDN_QK), BF16),
        jax.ShapeDtypeStruct((B, S, GDN_VW), BF16), jax.ShapeDtypeStruct((B, S, GDN_VW), BF16),
        jax.ShapeDtypeStruct((B, S, LANES), F32), jax.ShapeDtypeStruct((B, 2 * GDN_HEADS, S), F32),
        jax.ShapeDtypeStruct((B, MLA_HEADS, nt, ATT_DK, tm), BF16),
        jax.ShapeDtypeStruct((B, MLA_HEADS, S, ATT_DK), BF16),
        jax.ShapeDtypeStruct((B, MLA_HEADS, nt, VT_ROWS, tm), BF16),
    )
    transposed = lambda r: pl.BlockSpec((1, MLA_HEADS, 1, r, tm), lambda b, i: (b, 0, i, 0, 0))
    out_specs = (
        row(GDN_QK), row(GDN_QK), row(GDN_VW), row(GDN_VW), row(LANES),
        pl.BlockSpec((1, 2 * GDN_HEADS, tm), lambda b, i: (b, 0, i)),
        transposed(ATT_DK), head, transposed(VT_ROWS),
    )
    in_specs = [row(D)] + [_layer_spec(a, layer) for a in params] + [row(LANES), row(LANES)]
    kern = functools.partial(_prep_kernel, tm=tm,
                             att_scale=(MLA_NOPE + MLA_ROPE) ** -0.5 * float(np.log2(np.e)))
    return pl.pallas_call(
        kern, out_shape=out_shape, grid=(B, nt), in_specs=in_specs, out_specs=out_specs,
        scratch_shapes=[pltpu.VMEM((8, QKV_W), F32)],
        compiler_params=pltpu.CompilerParams(dimension_semantics=("arbitrary", "arbitrary"),
                                             vmem_limit_bytes=V7X_VMEM_LIMIT),
        name="prep",
    )(x, *params, cos_t, sin_t)


def _gdn_kernel(q_ref, k_ref, v_ref, z_ref, gbcol_ref, gbrow_ref, ng_ref, o_ref, state, *, nchunk,
                group):
    @pl.when(pl.program_id(1) == 0)
    def _():
        state[...] = jnp.zeros_like(state)

    C = CHUNK
    r = lax.broadcasted_iota(jnp.int32, (C, C), 0)
    c = lax.broadcasted_iota(jnp.int32, (C, C), 1)
    tri_incl = r >= c
    tri_strict = r > c
    eye = (r == c).astype(F32)
    grows = group * C
    rr = lax.broadcasted_iota(jnp.int32, (grows, grows), 0)
    cc = lax.broadcasted_iota(jnp.int32, (grows, grows), 1)
    same_chunk = (rr // C) == (cc // C)
    cum_down = (same_chunk & (rr >= cc)).astype(F32)
    cum_right = (same_chunk & (rr <= cc)).astype(F32)
    ng = ng_ref[...]

    heads = range(GDN_HEADS)

    def prepare(chunks, out):
        g0 = chunks[0] * C
        gb_grp = gbcol_ref[0, g0:g0 + grows, :]
        gcum_c = _dot(cum_down, gb_grp, HIGHEST)
        gcum_r = _dot(gbrow_ref[0, :, g0:g0 + grows], cum_right, HIGHEST)
        low, attn, rhs, qdec, kdec, egl = [], [], [], [], [], []
        for ci, hd in [(ci, hd) for ci in chunks for hd in heads]:
            r0, lo = ci * C, hd * GDN_DK
            gc = gcum_c[r0 - g0:r0 - g0 + C, G_LANE + hd:G_LANE + hd + 1]
            gr = gcum_r[GDN_HEADS + hd:GDN_HEADS + hd + 1, r0 - g0:r0 - g0 + C]
            gl = gc[C - 1:C, :]
            beta = gb_grp[r0 - g0:r0 - g0 + C, BETA_LANE + hd:BETA_LANE + hd + 1]
            decay = jnp.exp(jnp.where(tri_incl, gc - gr, -jnp.inf))
            eg = jnp.exp(gc)
            qh = q_ref[0, r0:r0 + C, lo:lo + GDN_DK].astype(F32)
            kh = k_ref[0, r0:r0 + C, lo:lo + GDN_DK].astype(F32)
            vh = v_ref[0, r0:r0 + C, lo:lo + GDN_DV].astype(F32)
            kb = kh * beta
            kq = _bdot_nt(jnp.concatenate([kb, qh], 0), kh)
            low.append(jnp.where(tri_strict, kq[:C] * decay, 0.0))
            attn.append(jnp.where(tri_incl, kq[C:] * decay, 0.0))
            rhs.append(jnp.concatenate([vh * beta, kb * eg], 1))
            qdec.append(qh * eg)
            kdec.append(kh * jnp.exp(gl - gc))
            egl.append(jnp.exp(gl))
        yield
        items = range(len(low))
        inv = [eye - low[it] for it in items]
        pw = [_bdot(low[it], low[it]) for it in items]
        yield
        span = 2
        while span < C:
            if 2 * span >= C:
                inv = [inv[it] + _bdot(inv[it], pw[it]) for it in items]
            else:
                both = [_bdot(jnp.concatenate([inv[it], pw[it]], 0), pw[it]) for it in items]
                inv = [inv[it] + both[it][:C] for it in items]
                pw = [both[it][C:] for it in items]
            span *= 2
            yield
        sol = [_bdot(inv[it], rhs[it]) for it in items]
        for n, ci in enumerate(chunks):
            sl = slice(n * GDN_HEADS, (n + 1) * GDN_HEADS)
            out[ci] = dict(u=[x[:, :GDN_DV] for x in sol[sl]], w=[x[:, GDN_DV:] for x in sol[sl]],
                           attn=attn[sl], qdec=qdec[sl], kdec=kdec[sl], egl=egl[sl])
        yield

    def recur_chunk(ci, pre, s):
        r0 = ci * C
        ws = [_bdot(jnp.concatenate([pre["w"][hd], pre["qdec"][hd]], 0), s[hd]) for hd in heads]
        yield
        v_new = [pre["u"][hd] - ws[hd][:C] for hd in heads]
        o = [ws[hd][C:] + _bdot(pre["attn"][hd], v_new[hd]) for hd in heads]
        for hd in heads:
            s[hd] = s[hd] * pre["egl"][hd] + _bdot_tn(pre["kdec"][hd], v_new[hd])
        yield
        for hd in heads:
            lo = hd * GDN_DV
            zh = z_ref[0, r0:r0 + C, lo:lo + GDN_DV].astype(F32)
            on = o[hd] * lax.rsqrt(jnp.mean(o[hd] * o[hd], -1, keepdims=True) + RMS_EPS) * ng
            o_ref[0, r0:r0 + C, lo:lo + GDN_DV] = (on * _silu(zh)).astype(BF16)
        yield

    def recur(chunks, prepared, s):
        for ci in chunks:
            yield from recur_chunk(ci, prepared[ci], s)

    s = [state[hd] for hd in heads]
    groups = [list(range(g, min(g + group, nchunk))) for g in range(0, nchunk, group)]
    prepared = {}
    for _ in prepare(groups[0], prepared):
        pass
    for gi, chunks in enumerate(groups):
        nxt = prepare(groups[gi + 1], prepared) if gi + 1 < len(groups) else iter(())
        cur = recur(chunks, prepared, s)
        nxt_done = cur_done = False
        while not (nxt_done and cur_done):
            nxt_done = nxt_done or next(nxt, "end") == "end"
            cur_done = cur_done or next(cur, "end") == "end"
    for hd in heads:
        state[hd] = s[hd]


def _gdn_call(qg, kg, vg, zg, gbcol, gbrow, layer, norm_g, *, rows, group):
    B, S, _ = qg.shape
    row = lambda w: pl.BlockSpec((1, rows, w), lambda b, i: (b, i, 0))
    kern = functools.partial(_gdn_kernel, nchunk=rows // CHUNK, group=group)
    return pl.pallas_call(
        kern, out_shape=jax.ShapeDtypeStruct((B, S, GDN_VW), BF16), grid=(B, S // rows),
        in_specs=[row(GDN_QK), row(GDN_QK), row(GDN_VW), row(GDN_VW), row(LANES),
                  pl.BlockSpec((1, 2 * GDN_HEADS, rows), lambda b, i: (b, 0, i)),
                  _layer_spec(norm_g, layer)],
        out_specs=row(GDN_VW),
        scratch_shapes=[pltpu.VMEM((GDN_HEADS, GDN_DK, GDN_DV), F32)],
        compiler_params=pltpu.CompilerParams(dimension_semantics=("arbitrary", "arbitrary"),
                                             vmem_limit_bytes=V7X_VMEM_LIMIT),
        name="gdn",
    )(qg, kg, vg, zg, gbcol, gbrow, norm_g)


def _attn_kernel(qt_ref, k_ref, vt_ref, o_ref, m_ref, acc_ref, *, blk, hps):
    i = pl.program_id(2)
    m_ref[...] = jnp.full_like(m_ref, -jnp.inf)
    acc_ref[...] = jnp.zeros_like(acc_ref)

    def step(j, masked):
        start = pl.multiple_of(j * blk, blk)
        st = [_dot(k_ref[0, hd, pl.ds(start, blk), :], qt_ref[0, hd, 0]) for hd in range(hps)]
        if masked:
            kc = lax.broadcasted_iota(jnp.int32, (blk, blk), 0) // CHUNK
            qc = lax.broadcasted_iota(jnp.int32, (blk, blk), 1) // CHUNK
            st = [jnp.where(kc <= qc, s, -jnp.inf) for s in st]
        for hd in range(hps):
            m_old = m_ref[hd]
            m_new = jnp.maximum(m_old, jnp.max(st[hd], axis=0, keepdims=True))
            alpha = jnp.exp2(m_old - m_new)
            p = jnp.exp2(st[hd] - m_new)
            acc_ref[hd] = alpha * acc_ref[hd] + _dot(vt_ref[0, hd, j], p.astype(BF16))
            m_ref[hd] = m_new

    step(i, True)
    lax.fori_loop(0, i, lambda j, carry: (step(j, False), carry)[1], 0)
    for hd in range(hps):
        out = acc_ref[hd, 0:MLA_V, :] / acc_ref[hd, MLA_V:MLA_V + 1, :]
        o_ref[0, :, hd * MLA_V:(hd + 1) * MLA_V] = out.T.astype(BF16)


def _attn_call(qt, ka, vt, *, blk, hps):
    B, H, S, _ = ka.shape
    nb = S // blk
    kern = functools.partial(_attn_kernel, blk=blk, hps=hps)
    return pl.pallas_call(
        kern, out_shape=jax.ShapeDtypeStruct((B, S, H * MLA_V), BF16), grid=(B, H // hps, nb),
        in_specs=[pl.BlockSpec((1, hps, 1, ATT_DK, blk), lambda b, h, i: (b, h, i, 0, 0)),
                  pl.BlockSpec((1, hps, S, ATT_DK), lambda b, h, i: (b, h, 0, 0),
                               pipeline_mode=pl.Buffered(1)),
                  pl.BlockSpec((1, hps, nb, VT_ROWS, blk), lambda b, h, i: (b, h, 0, 0, 0),
                               pipeline_mode=pl.Buffered(1))],
        out_specs=pl.BlockSpec((1, blk, hps * MLA_V), lambda b, h, i: (b, i, h)),
        scratch_shapes=[pltpu.VMEM((hps, 1, blk), F32), pltpu.VMEM((hps, VT_ROWS, blk), F32)],
        compiler_params=pltpu.CompilerParams(
            dimension_semantics=("arbitrary", "arbitrary", "arbitrary"),
            vmem_limit_bytes=V7X_VMEM_LIMIT),
        name="attn",
    )(qt, ka, vt)


def _layer_norm(x, g, b):
    mu = jnp.mean(x, -1, keepdims=True)
    xc = x - mu
    var = jnp.mean(xc * xc, -1, keepdims=True)
    return xc * lax.rsqrt(var + LN_EPS) * g + b


def _post_kernel(x_ref, og_ref, om_ref, p_ref, wog_ref, wom_ref, l1g_ref, l1b_ref, wg_ref, wu_ref,
                 wd_ref, l2g_ref, l2b_ref, wple_ref, wpg_ref, o_ref, *, alpha, ff_chunk):
    mix = _dot(og_ref[...], wog_ref[...]) + _dot(om_ref[...], wom_ref[...])
    x1 = _layer_norm(alpha * x_ref[...] + mix, l1g_ref[...], l1b_ref[...])
    x1b = x1.astype(BF16)
    d_ff = wd_ref.shape[0]
    ff = jnp.zeros_like(x1)
    for c0 in range(0, d_ff, ff_chunk):
        c1 = min(c0 + ff_chunk, d_ff)
        gate = _dot(x1b, wg_ref[:, c0:c1])
        up = _dot(x1b, wu_ref[:, c0:c1])
        ff = ff + _dot((_silu(gate) * up).astype(BF16), wd_ref[c0:c1, :])
    x2 = _layer_norm(alpha * x1 + ff, l2g_ref[...], l2b_ref[...])
    emb = _dot(p_ref[...].astype(BF16), wple_ref[...])
    o_ref[...] = x2 + _sigmoid(_dot(x2.astype(BF16), wpg_ref[...])) * emb


def _post_call(x2d, og, om, p3d, layer, w_out, ln1_g, ln1_b, w_gate_up, w_down, ln2_g, ln2_b, w_ple,
               w_pg, *, tm, alpha, ff_chunk):
    T, D = x2d.shape
    row = lambda w: pl.BlockSpec((tm, w), lambda i: (i, 0))
    weights = ((w_out, (0, 2, 0)), (w_out, (0, 2, 1)), (ln1_g, None), (ln1_b, None),
               (w_gate_up, (1, 2, 0)), (w_gate_up, (1, 2, 1)), (w_down, None), (ln2_g, None),
               (ln2_b, None), (w_ple, None), (w_pg, None))
    kern = functools.partial(_post_kernel, alpha=alpha, ff_chunk=ff_chunk)
    return pl.pallas_call(
        kern, out_shape=jax.ShapeDtypeStruct((T, D), F32), grid=(T // tm,),
        in_specs=[row(D), row(og.shape[1]), row(om.shape[1]),
                  pl.BlockSpec((None, tm, p3d.shape[2]), lambda i: (layer, i, 0))]
        + [_layer_spec(w, layer, split) for w, split in weights],
        out_specs=row(D),
        compiler_params=pltpu.CompilerParams(dimension_semantics=("arbitrary",),
                                             vmem_limit_bytes=V7X_VMEM_LIMIT),
        name="post",
    )(x2d, og, om, p3d, *[w for w, _ in weights])


def _reorder_weights(w_in, w_uq, w_ukv, a_log, dt_bias):
    depth, d, _ = w_in.shape
    q, k, v, z, b, a, cq, ckv, kr = jnp.split(
        w_in, np.cumsum((GDN_QK, GDN_QK, GDN_VW, GDN_VW, GDN_HEADS, GDN_HEADS, Q_LORA, KV_LORA))
        .tolist(), axis=-1)
    pad = jnp.zeros((depth, d, LANES - MLA_ROPE - 2 * GDN_HEADS), w_in.dtype)
    w_in_r = jnp.concatenate([q, k, v, z, cq, ckv, kr, b, a, pad], -1).astype(BF16)

    uq = w_uq.reshape(depth, Q_LORA, MLA_HEADS, MLA_NOPE + MLA_ROPE)
    uq_nope = uq[..., :MLA_NOPE].reshape(depth, Q_LORA, MLA_HEADS * MLA_NOPE)
    lane_pad = ((0, 0), (0, 0), (0, 0), (0, LANES - MLA_ROPE))
    rope_cols = uq[..., MLA_NOPE:]
    partner_cols = jnp.concatenate([rope_cols[..., MLA_ROPE // 2:], rope_cols[..., :MLA_ROPE // 2]], -1)
    w_uq_r = jnp.concatenate(
        [uq_nope, jnp.pad(rope_cols, lane_pad).reshape(depth, Q_LORA, MLA_HEADS * LANES),
         jnp.pad(partner_cols, lane_pad).reshape(depth, Q_LORA, MLA_HEADS * LANES)], -1).astype(BF16)

    ukv = w_ukv.reshape(depth, KV_LORA, MLA_HEADS, MLA_NOPE + MLA_V)
    w_ukv_r = jnp.concatenate([ukv[..., :MLA_NOPE].reshape(depth, KV_LORA, -1),
                               ukv[..., MLA_NOPE:].reshape(depth, KV_LORA, -1)], -1).astype(BF16)

    lanes_pad = ((0, 0), (0, 0), (G_LANE, LANES - G_LANE - GDN_HEADS))
    alog_pad = jnp.pad(a_log[:, None, :], lanes_pad)
    dtb_pad = jnp.pad(dt_bias[:, None, :], lanes_pad)
    return w_in_r, w_uq_r, w_ukv_r, alog_pad, dtb_pad


def _rope_tables(positions):
    inv_freq = ROPE_THETA ** (-jnp.arange(0, MLA_ROPE, 2, dtype=F32) / MLA_ROPE)
    ang = positions.astype(F32)[..., None] * inv_freq
    cos, sin = jnp.cos(ang), jnp.sin(ang)
    reps = LANES // MLA_ROPE
    return (jnp.tile(jnp.concatenate([cos, cos], -1), (1, 1, reps)),
            jnp.tile(jnp.concatenate([-sin, sin], -1), (1, 1, reps)))


def _tile_rows(s, want):
    t = min(want, s)
    while s % t:
        t //= 2
    return t


def kernel(x, p, positions, w_in, conv_w, a_log, dt_bias, gdn_norm_g, q_norm_g, w_uq, kv_norm_g,
           w_ukv, w_out, ln1_g, ln1_b, w_gate_up, w_down, ln2_g, ln2_b, w_ple, w_ple_gate):
    B, S, D = x.shape
    depth = w_in.shape[0]
    d_ff = w_down.shape[1]
    alpha = (2.0 * depth) ** 0.25
    tm = _tile_rows(S, 512)
    gdn_rows = _tile_rows(S, 1024)
    ff_chunk = -(-d_ff // (2 * V7X_MXU_COLS)) * V7X_MXU_COLS

    w_in_r, w_uq_r, w_ukv_r, alog_pad, dtb_pad = _reorder_weights(w_in, w_uq, w_ukv, a_log, dt_bias)
    cos_t, sin_t = _rope_tables(positions)
    w_out_b = w_out.astype(BF16)
    w_gu_b = w_gate_up.astype(BF16)
    w_down_b = w_down.astype(BF16)
    w_ple_b = w_ple.astype(BF16)
    w_pg_b = w_ple_gate.astype(BF16)
    vec = lambda a: a[:, None, :]
    p3d = p.reshape(depth, B * S, -1)

    for i in range(depth):
        qg, kg, vg, zg, gbcol, gbrow, qt, ka, vt = _prep_call(
            x, i, w_in_r, conv_w, alog_pad, dtb_pad, vec(q_norm_g), vec(kv_norm_g), w_uq_r, w_ukv_r,
            cos_t, sin_t, tm=tm)
        og = _gdn_call(qg, kg, vg, zg, gbcol, gbrow, i, vec(gdn_norm_g), rows=gdn_rows, group=4)
        om = _attn_call(qt, ka, vt, blk=tm, hps=MLA_HEADS)
        x = _post_call(
            x.reshape(B * S, D), og.reshape(B * S, -1), om.reshape(B * S, -1), p3d, i, w_out_b,
            vec(ln1_g), vec(ln1_b), w_gu_b, w_down_b, vec(ln2_g), vec(ln2_b), w_ple_b, w_pg_b,
            tm=tm, alpha=alpha, ff_chunk=ff_chunk,
        ).reshape(B, S, D)
    return x
```

```python
import functools

import jax
import jax.numpy as jnp
import numpy as np
from jax import lax
from jax.experimental import pallas as pl
from jax.experimental.pallas import tpu as pltpu

F32 = jnp.float32
BF16 = jnp.bfloat16

CHUNK = 64
GDN_HEADS = 4
GDN_DK = 128
GDN_DV = 128
CONV_WIDTH = 4
MLA_HEADS = 4
MLA_NOPE = 128
MLA_ROPE = 64
MLA_V = 128
Q_LORA = 384
KV_LORA = 256
ROPE_THETA = 10000.0
LN_EPS = 1e-5
RMS_EPS = 1e-6

GDN_QK = GDN_HEADS * GDN_DK
GDN_VW = GDN_HEADS * GDN_DV
QKV_W = 2 * GDN_QK + GDN_VW
LANES = 128
Z_OFF = QKV_W
CQ_OFF = Z_OFF + GDN_VW
CKV_OFF = CQ_OFF + Q_LORA
LAST_OFF = CKV_OFF + KV_LORA
IN_PAD = LAST_OFF + LANES
BETA_LANE = MLA_ROPE
G_LANE = MLA_ROPE + GDN_HEADS
ATT_DK = 2 * LANES
VT_ROWS = MLA_V + 16
CONV_COLS = GDN_QK

V7X_VMEM_LIMIT = 56 * 1024 * 1024
V7X_MXU_COLS = 256

HIGHEST = lax.Precision.HIGHEST


def _dot(a, b, precision=None):
    return jnp.dot(a, b, preferred_element_type=F32, precision=precision)


def _dot_nt(a, b, precision=None):
    return lax.dot_general(a, b, (((1,), (1,)), ((), ())), preferred_element_type=F32,
                           precision=precision)


def _dot_tn(a, b, precision=None):
    return lax.dot_general(a, b, (((0,), (0,)), ((), ())), preferred_element_type=F32,
                           precision=precision)


def _bdot(a, b):
    return _dot(a.astype(BF16), b.astype(BF16))


def _bdot_nt(a, b):
    return _dot_nt(a.astype(BF16), b.astype(BF16))


def _bdot_tn(a, b):
    return _dot_tn(a.astype(BF16), b.astype(BF16))


def _sigmoid(x):
    return 1.0 / (1.0 + jnp.exp(-x))


def _silu(x):
    return x * _sigmoid(x)


def _softplus(x):
    return jnp.maximum(x, 0.0) + jnp.log(1.0 + jnp.exp(-jnp.abs(x)))


def _layer_spec(arr, layer, split=None):
    block = list(arr.shape[1:])
    index = [0] * len(block)
    if split is not None:
        axis, parts, which = split
        block[axis] //= parts
        index[axis] = which
    return pl.BlockSpec((None, *block), lambda *_: (layer, *index), pipeline_mode=pl.Buffered(1))


def _prep_kernel(x_ref, w_in_ref, conv_w_ref, alog_ref, dtb_ref, qng_ref, kvng_ref, w_uq_ref,
                 w_ukv_ref, cos_ref, sin_ref,
                 qg_ref, kg_ref, vg_ref, zg_ref, gbcol_ref, gbrow_ref, qt_ref, ka_ref, vt_ref,
                 cbuf, *, tm, att_scale):
    @pl.when(pl.program_id(1) == 0)
    def _():
        cbuf[...] = jnp.zeros_like(cbuf)

    row8 = lax.broadcasted_iota(jnp.int32, (8, CONV_COLS), 0)

    xb = x_ref[0].astype(BF16)
    lane = lax.broadcasted_iota(jnp.int32, (tm, LANES), 1)
    cos = cos_ref[0]
    sin = sin_ref[0]
    first_half = (lane % MLA_ROPE) < (MLA_ROPE // 2)
    nh = MLA_HEADS * MLA_NOPE

    def proj(c0, c1):
        return _dot(xb, w_in_ref[:, c0:c1])

    def rope(blk):
        partner = jnp.where(first_half, pltpu.roll(blk, LANES - MLA_ROPE // 2, 1),
                            pltpu.roll(blk, MLA_ROPE // 2, 1))
        return blk * cos + partner * sin

    def conv_group(grp, part, out_ref):
        c0 = grp * GDN_QK + part * CONV_COLS
        c1 = c0 + CONV_COLS
        o0 = part * CONV_COLS
        pre = proj(c0, c1)
        cw = conv_w_ref[:, c0:c1]
        y = pre * cw[0:1, :]
        for j in range(1, CONV_WIDTH):
            carry_in = cbuf[j - 1:j, c0:c1]
            cbuf[j - 1:j, c0:c1] = y[tm - 1:tm, :]
            shifted = pltpu.roll(y, 1, 0)
            top = jnp.where(row8 == 0, carry_in, shifted[0:8, :])
            y = jnp.concatenate([top, shifted[8:, :]], 0) + pre * cw[j:j + 1, :]
        y = _silu(y)
        if out_ref is vg_ref:
            out_ref[0, :, o0:o0 + CONV_COLS] = y.astype(BF16)
            return
        out_scale = GDN_DK ** -0.5 if out_ref is qg_ref else 1.0
        for lo in range(0, CONV_COLS, GDN_DK):
            yh = y[:, lo:lo + GDN_DK]
            yn = yh * (lax.rsqrt(jnp.sum(yh * yh, -1, keepdims=True) + RMS_EPS) * out_scale)
            out_ref[0, :, o0 + lo:o0 + lo + GDN_DK] = yn.astype(BF16)

    for grp, out_ref in enumerate((qg_ref, kg_ref, vg_ref)):
        for part in range(GDN_QK // CONV_COLS):
            conv_group(grp, part, out_ref)

    latent = proj(CQ_OFF, IN_PAD)
    cq = latent[:, :Q_LORA]
    cqn = cq * lax.rsqrt(jnp.mean(cq * cq, -1, keepdims=True) + RMS_EPS) * qng_ref[...]
    qm = _dot(cqn.astype(BF16), w_uq_ref[...])
    for hd in range(MLA_HEADS):
        lo = hd * LANES
        q_rope = qm[:, nh + lo:nh + lo + LANES] * cos + qm[:, 2 * nh + lo:2 * nh + lo + LANES] * sin
        q_full = jnp.concatenate([qm[:, lo:lo + LANES], q_rope], -1) * att_scale
        qt_ref[0, hd, 0] = q_full.T.astype(BF16)

    ckv = latent[:, Q_LORA:Q_LORA + KV_LORA]
    last = latent[:, Q_LORA + KV_LORA:]
    ckvn = ckv * lax.rsqrt(jnp.mean(ckv * ckv, -1, keepdims=True) + RMS_EPS) * kvng_ref[...]
    kvm = _dot(ckvn.astype(BF16), w_ukv_ref[...])
    k_rope = jnp.where(lane < MLA_ROPE, rope(last), 0.0)
    ones_rows = (lax.broadcasted_iota(jnp.int32, (VT_ROWS - MLA_V, tm), 0) == 0).astype(BF16)
    for hd in range(MLA_HEADS):
        lo = hd * LANES
        ka_ref[0, hd] = jnp.concatenate([kvm[:, lo:lo + LANES], k_rope], -1).astype(BF16)
        vt_ref[0, hd, 0, 0:MLA_V, :] = kvm[:, nh + lo:nh + lo + LANES].T.astype(BF16)
        vt_ref[0, hd, 0, MLA_V:VT_ROWS, :] = ones_rows

    beta = _sigmoid(last)
    g = -jnp.exp(alog_ref[...]) * _softplus(last + dtb_ref[...])
    gb = jnp.where((lane >= BETA_LANE) & (lane < G_LANE), beta,
                   jnp.where((lane >= G_LANE) & (lane < G_LANE + GDN_HEADS), g, 0.0))
    gbcol_ref[0] = gb
    gbrow_ref[0] = gb.T[BETA_LANE:BETA_LANE + 2 * GDN_HEADS, :]

    zg_ref[0] = proj(Z_OFF, Z_OFF + GDN_VW).astype(BF16)


def _prep_call(x, layer, w_in_r, conv_w, alog_pad, dtb_pad, qn_g, kvn_g, w_uq_r, w_ukv_r, cos_t,
               sin_t, *, tm):
    B, S, D = x.shape
    params = (w_in_r, conv_w, alog_pad, dtb_pad, qn_g, kvn_g, w_uq_r, w_ukv_r)
    nt = S // tm
    row = lambda w: pl.BlockSpec((1, tm, w), lambda b, i: (b, i, 0))
    head = pl.BlockSpec((1, MLA_HEADS, tm, ATT_DK), lambda b, i: (b, 0, i, 0))
    out_shape = (
        jax.ShapeDtypeStruct((B, S, GDN_QK), BF16), jax.ShapeDtypeStruct((B, S, GDN_QK), BF16),
        jax.ShapeDtypeStruct((B, S, GDN_VW), BF16), jax.ShapeDtypeStruct((B, S, GDN_VW), BF16),
        jax.ShapeDtypeStruct((B, S, LANES), F32), jax.ShapeDtypeStruct((B, 2 * GDN_HEADS, S), F32),
        jax.ShapeDtypeStruct((B, MLA_HEADS, nt, ATT_DK, tm), BF16),
        jax.ShapeDtypeStruct((B, MLA_HEADS, S, ATT_DK), BF16),
        jax.ShapeDtypeStruct((B, MLA_HEADS, nt, VT_ROWS, tm), BF16),
    )
    transposed = lambda r: pl.BlockSpec((1, MLA_HEADS, 1, r, tm), lambda b, i: (b, 0, i, 0, 0))
    out_specs = (
        row(GDN_QK), row(GDN_QK), row(GDN_VW), row(GDN_VW), row(LANES),
        pl.BlockSpec((1, 2 * GDN_HEADS, tm), lambda b, i: (b, 0, i)),
        transposed(ATT_DK), head, transposed(VT_ROWS),
    )
    in_specs = [row(D)] + [_layer_spec(a, layer) for a in params] + [row(LANES), row(LANES)]
    kern = functools.partial(_prep_kernel, tm=tm,
                             att_scale=(MLA_NOPE + MLA_ROPE) ** -0.5 * float(np.log2(np.e)))
    return pl.pallas_call(
        kern, out_shape=out_shape, grid=(B, nt), in_specs=in_specs, out_specs=out_specs,
        scratch_shapes=[pltpu.VMEM((8, QKV_W), F32)],
        compiler_params=pltpu.CompilerParams(dimension_semantics=("arbitrary", "arbitrary"),
                                             vmem_limit_bytes=V7X_VMEM_LIMIT),
        name="prep",
    )(x, *params, cos_t, sin_t)


def _gdn_kernel(q_ref, k_ref, v_ref, z_ref, gbcol_ref, gbrow_ref, ng_ref, o_ref, state, *, nchunk,
                group):
    @pl.when(pl.program_id(1) == 0)
    def _():
        state[...] = jnp.zeros_like(state)

    C = CHUNK
    r = lax.broadcasted_iota(jnp.int32, (C, C), 0)
    c = lax.broadcasted_iota(jnp.int32, (C, C), 1)
    tri_incl = r >= c
    tri_strict = r > c
    eye = (r == c).astype(F32)
    grows = group * C
    rr = lax.broadcasted_iota(jnp.int32, (grows, grows), 0)
    cc = lax.broadcasted_iota(jnp.int32, (grows, grows), 1)
    same_chunk = (rr // C) == (cc // C)
    cum_down = (same_chunk & (rr >= cc)).astype(F32)
    cum_right = (same_chunk & (rr <= cc)).astype(F32)
    ng = ng_ref[...]

    heads = range(GDN_HEADS)

    def prepare(chunks, out):
        g0 = chunks[0] * C
        gb_grp = gbcol_ref[0, g0:g0 + grows, :]
        gcum_c = _dot(cum_down, gb_grp, HIGHEST)
        gcum_r = _dot(gbrow_ref[0, :, g0:g0 + grows], cum_right, HIGHEST)
        low, attn, rhs, qdec, kdec, egl = [], [], [], [], [], []
        for ci, hd in [(ci, hd) for ci in chunks for hd in heads]:
            r0, lo = ci * C, hd * GDN_DK
            gc = gcum_c[r0 - g0:r0 - g0 + C, G_LANE + hd:G_LANE + hd + 1]
            gr = gcum_r[GDN_HEADS + hd:GDN_HEADS + hd + 1, r0 - g0:r0 - g0 + C]
            gl = gc[C - 1:C, :]
            beta = gb_grp[r0 - g0:r0 - g0 + C, BETA_LANE + hd:BETA_LANE + hd + 1]
            decay = jnp.exp(jnp.where(tri_incl, gc - gr, -jnp.inf))
            eg = jnp.exp(gc)
            qh = q_ref[0, r0:r0 + C, lo:lo + GDN_DK].astype(F32)
            kh = k_ref[0, r0:r0 + C, lo:lo + GDN_DK].astype(F32)
            vh = v_ref[0, r0:r0 + C, lo:lo + GDN_DV].astype(F32)
            kb = kh * beta
            kq = _bdot_nt(jnp.concatenate([kb, qh], 0), kh)
            low.append(jnp.where(tri_strict, kq[:C] * decay, 0.0))
            attn.append(jnp.where(tri_incl, kq[C:] * decay, 0.0))
            rhs.append(jnp.concatenate([vh * beta, kb * eg], 1))
            qdec.append(qh * eg)
            kdec.append(kh * jnp.exp(gl - gc))
            egl.append(jnp.exp(gl))
        yield
        items = range(len(low))
        inv = [eye - low[it] for it in items]
        pw = [_bdot(low[it], low[it]) for it in items]
        yield
        span = 2
        while span < C:
            if 2 * span >= C:
                inv = [inv[it] + _bdot(inv[it], pw[it]) for it in items]
            else:
                both = [_bdot(jnp.concatenate([inv[it], pw[it]], 0), pw[it]) for it in items]
                inv = [inv[it] + both[it][:C] for it in items]
                pw = [both[it][C:] for it in items]
            span *= 2
            yield
        sol = [_bdot(inv[it], rhs[it]) for it in items]
        for n, ci in enumerate(chunks):
            sl = slice(n * GDN_HEADS, (n + 1) * GDN_HEADS)
            out[ci] = dict(u=[x[:, :GDN_DV] for x in sol[sl]], w=[x[:, GDN_DV:] for x in sol[sl]],
                           attn=attn[sl], qdec=qdec[sl], kdec=kdec[sl], egl=egl[sl])
        yield

    def recur_chunk(ci, pre, s):
        r0 = ci * C
        ws = [_bdot(jnp.concatenate([pre["w"][hd], pre["qdec"][hd]], 0), s[hd]) for hd in heads]
        yield
        v_new = [pre["u"][hd] - ws[hd][:C] for hd in heads]
        o = [ws[hd][C:] + _bdot(pre["attn"][hd], v_new[hd]) for hd in heads]
        for hd in heads:
            s[hd] = s[hd] * pre["egl"][hd] + _bdot_tn(pre["kdec"][hd], v_new[hd])
        yield
        for hd in heads:
            lo = hd * GDN_DV
            zh = z_ref[0, r0:r0 + C, lo:lo + GDN_DV].astype(F32)
            on = o[hd] * lax.rsqrt(jnp.mean(o[hd] * o[hd], -1, keepdims=True) + RMS_EPS) * ng
            o_ref[0, r0:r0 + C, lo:lo + GDN_DV] = (on * _silu(zh)).astype(BF16)
        yield

    def recur(chunks, prepared, s):
        for ci in chunks:
            yield from recur_chunk(ci, prepared[ci], s)

    s = [state[hd] for hd in heads]
    groups = [list(range(g, min(g + group, nchunk))) for g in range(0, nchunk, group)]
    prepared = {}
    for _ in prepare(groups[0], prepared):
        pass
    for gi, chunks in enumerate(groups):
        nxt = prepare(groups[gi + 1], prepared) if gi + 1 < len(groups) else iter(())
        cur = recur(chunks, prepared, s)
        nxt_done = cur_done = False
        while not (nxt_done and cur_done):
            nxt_done = nxt_done or next(nxt, "end") == "end"
            cur_done = cur_done or next(cur, "end") == "end"
    for hd in heads:
        state[hd] = s[hd]


def _gdn_call(qg, kg, vg, zg, gbcol, gbrow, layer, norm_g, *, rows, group):
    B, S, _ = qg.shape
    row = lambda w: pl.BlockSpec((1, rows, w), lambda b, i: (b, i, 0))
    kern = functools.partial(_gdn_kernel, nchunk=rows // CHUNK, group=group)
    return pl.pallas_call(
        kern, out_shape=jax.ShapeDtypeStruct((B, S, GDN_VW), BF16), grid=(B, S // rows),
        in_specs=[row(GDN_QK), row(GDN_QK), row(GDN_VW), row(GDN_VW), row(LANES),
                  pl.BlockSpec((1, 2 * GDN_HEADS, rows), lambda b, i: (b, 0, i)),
                  _layer_spec(norm_g, layer)],
        out_specs=row(GDN_VW),
        scratch_shapes=[pltpu.VMEM((GDN_HEADS, GDN_DK, GDN_DV), F32)],
        compiler_params=pltpu.CompilerParams(dimension_semantics=("arbitrary", "arbitrary"),
                                             vmem_limit_bytes=V7X_VMEM_LIMIT),
        name="gdn",
    )(qg, kg, vg, zg, gbcol, gbrow, norm_g)


def _attn_kernel(qt_ref, k_ref, vt_ref, o_ref, m_ref, acc_ref, st_ref, *, blk, hps):
    i = pl.program_id(2)
    m_ref[...] = jnp.full_like(m_ref, -jnp.inf)
    acc_ref[...] = jnp.zeros_like(acc_ref)

    def scores(hd, j, masked=False):
        start = pl.multiple_of(j * blk, blk)
        s = _dot(k_ref[0, hd, pl.ds(start, blk), :], qt_ref[0, hd, 0]).astype(BF16)
        if masked:
            kc = lax.broadcasted_iota(jnp.int32, (blk, blk), 0) // CHUNK
            qc = lax.broadcasted_iota(jnp.int32, (blk, blk), 1) // CHUNK
            s = jnp.where(kc <= qc, s, -jnp.inf)
        st_ref[hd] = s

    def process(j, nxt):
        for hd in range(hps):
            s = st_ref[hd]
            m_old = m_ref[hd]
            m_new = jnp.maximum(m_old, jnp.max(s, axis=0, keepdims=True).astype(F32))
            alpha = jnp.exp2(m_old - m_new)
            p = jnp.exp2(s - m_new.astype(BF16))
            acc_ref[hd] = alpha * acc_ref[hd] + _dot(vt_ref[0, hd, j], p)
            m_ref[hd] = m_new
            if nxt is not None:
                scores(hd, nxt)

    for hd in range(hps):
        scores(hd, i, masked=True)
    process(i, 0)

    @pl.when(i > 0)
    def _():
        lax.fori_loop(0, i - 1, lambda j, carry: (process(j, j + 1), carry)[1], 0)
        process(i - 1, None)

    for hd in range(hps):
        out = acc_ref[hd, 0:MLA_V, :] / acc_ref[hd, MLA_V:MLA_V + 1, :]
        o_ref[0, :, hd * MLA_V:(hd + 1) * MLA_V] = out.T.astype(BF16)


def _attn_call(qt, ka, vt, *, blk, hps):
    B, H, S, _ = ka.shape
    nb = S // blk
    kern = functools.partial(_attn_kernel, blk=blk, hps=hps)
    return pl.pallas_call(
        kern, out_shape=jax.ShapeDtypeStruct((B, S, H * MLA_V), BF16), grid=(B, H // hps, nb),
        in_specs=[pl.BlockSpec((1, hps, 1, ATT_DK, blk), lambda b, h, i: (b, h, i, 0, 0)),
                  pl.BlockSpec((1, hps, S, ATT_DK), lambda b, h, i: (b, h, 0, 0),
                               pipeline_mode=pl.Buffered(1)),
                  pl.BlockSpec((1, hps, nb, VT_ROWS, blk), lambda b, h, i: (b, h, 0, 0, 0),
                               pipeline_mode=pl.Buffered(1))],
        out_specs=pl.BlockSpec((1, blk, hps * MLA_V), lambda b, h, i: (b, i, h)),
        scratch_shapes=[pltpu.VMEM((hps, 1, blk), F32), pltpu.VMEM((hps, VT_ROWS, blk), F32),
                        pltpu.VMEM((hps, blk, blk), BF16)],
        compiler_params=pltpu.CompilerParams(
            dimension_semantics=("arbitrary", "arbitrary", "arbitrary"),
            vmem_limit_bytes=V7X_VMEM_LIMIT),
        name="attn",
    )(qt, ka, vt)


def _layer_norm(x, g, b):
    mu = jnp.mean(x, -1, keepdims=True)
    xc = x - mu
    var = jnp.mean(xc * xc, -1, keepdims=True)
    return xc * lax.rsqrt(var + LN_EPS) * g + b


def _post_kernel(x_ref, og_ref, om_ref, p_ref, wog_ref, wom_ref, l1g_ref, l1b_ref, wg_ref, wu_ref,
                 wd_ref, l2g_ref, l2b_ref, wple_ref, wpg_ref, o_ref, *, alpha, ff_chunk):
    mix = _dot(og_ref[...], wog_ref[...]) + _dot(om_ref[...], wom_ref[...])
    x1 = _layer_norm(alpha * x_ref[...] + mix, l1g_ref[...], l1b_ref[...])
    x1b = x1.astype(BF16)
    d_ff = wd_ref.shape[0]
    ff = jnp.zeros_like(x1)
    for c0 in range(0, d_ff, ff_chunk):
        c1 = min(c0 + ff_chunk, d_ff)
        gate = _dot(x1b, wg_ref[:, c0:c1])
        up = _dot(x1b, wu_ref[:, c0:c1])
        ff = ff + _dot((_silu(gate) * up).astype(BF16), wd_ref[c0:c1, :])
    x2 = _layer_norm(alpha * x1 + ff, l2g_ref[...], l2b_ref[...])
    emb = _dot(p_ref[...].astype(BF16), wple_ref[...])
    o_ref[...] = x2 + _sigmoid(_dot(x2.astype(BF16), wpg_ref[...])) * emb


def _post_call(x2d, og, om, p3d, layer, w_out, ln1_g, ln1_b, w_gate_up, w_down, ln2_g, ln2_b, w_ple,
               w_pg, *, tm, alpha, ff_chunk):
    T, D = x2d.shape
    row = lambda w: pl.BlockSpec((tm, w), lambda i: (i, 0))
    weights = ((w_out, (0, 2, 0)), (w_out, (0, 2, 1)), (ln1_g, None), (ln1_b, None),
               (w_gate_up, (1, 2, 0)), (w_gate_up, (1, 2, 1)), (w_down, None), (ln2_g, None),
               (ln2_b, None), (w_ple, None), (w_pg, None))
    kern = functools.partial(_post_kernel, alpha=alpha, ff_chunk=ff_chunk)
    return pl.pallas_call(
        kern, out_shape=jax.ShapeDtypeStruct((T, D), F32), grid=(T // tm,),
        in_specs=[row(D), row(og.shape[1]), row(om.shape[1]),
                  pl.BlockSpec((None, tm, p3d.shape[2]), lambda i: (layer, i, 0))]
        + [_layer_spec(w, layer, split) for w, split in weights],
        out_specs=row(D),
        compiler_params=pltpu.CompilerParams(dimension_semantics=("arbitrary",),
                                             vmem_limit_bytes=V7X_VMEM_LIMIT),
        name="post",
    )(x2d, og, om, p3d, *[w for w, _ in weights])


def _reorder_weights(w_in, w_uq, w_ukv, a_log, dt_bias):
    depth, d, _ = w_in.shape
    q, k, v, z, b, a, cq, ckv, kr = jnp.split(
        w_in, np.cumsum((GDN_QK, GDN_QK, GDN_VW, GDN_VW, GDN_HEADS, GDN_HEADS, Q_LORA, KV_LORA))
        .tolist(), axis=-1)
    pad = jnp.zeros((depth, d, LANES - MLA_ROPE - 2 * GDN_HEADS), w_in.dtype)
    w_in_r = jnp.concatenate([q, k, v, z, cq, ckv, kr, b, a, pad], -1).astype(BF16)

    uq = w_uq.reshape(depth, Q_LORA, MLA_HEADS, MLA_NOPE + MLA_ROPE)
    uq_nope = uq[..., :MLA_NOPE].reshape(depth, Q_LORA, MLA_HEADS * MLA_NOPE)
    lane_pad = ((0, 0), (0, 0), (0, 0), (0, LANES - MLA_ROPE))
    rope_cols = uq[..., MLA_NOPE:]
    partner_cols = jnp.concatenate([rope_cols[..., MLA_ROPE // 2:], rope_cols[..., :MLA_ROPE // 2]], -1)
    w_uq_r = jnp.concatenate(
        [uq_nope, jnp.pad(rope_cols, lane_pad).reshape(depth, Q_LORA, MLA_HEADS * LANES),
         jnp.pad(partner_cols, lane_pad).reshape(depth, Q_LORA, MLA_HEADS * LANES)], -1).astype(BF16)

    ukv = w_ukv.reshape(depth, KV_LORA, MLA_HEADS, MLA_NOPE + MLA_V)
    w_ukv_r = jnp.concatenate([ukv[..., :MLA_NOPE].reshape(depth, KV_LORA, -1),
                               ukv[..., MLA_NOPE:].reshape(depth, KV_LORA, -1)], -1).astype(BF16)

    lanes_pad = ((0, 0), (0, 0), (G_LANE, LANES - G_LANE - GDN_HEADS))
    alog_pad = jnp.pad(a_log[:, None, :], lanes_pad)
    dtb_pad = jnp.pad(dt_bias[:, None, :], lanes_pad)
    return w_in_r, w_uq_r, w_ukv_r, alog_pad, dtb_pad


def _rope_tables(positions):
    inv_freq = ROPE_THETA ** (-jnp.arange(0, MLA_ROPE, 2, dtype=F32) / MLA_ROPE)
    ang = positions.astype(F32)[..., None] * inv_freq
    cos, sin = jnp.cos(ang), jnp.sin(ang)
    reps = LANES // MLA_ROPE
    return (jnp.tile(jnp.concatenate([cos, cos], -1), (1, 1, reps)),
            jnp.tile(jnp.concatenate([-sin, sin], -1), (1, 1, reps)))


def _tile_rows(s, want):
    t = min(want, s)
    while s % t:
        t //= 2
    return t


def kernel(x, p, positions, w_in, conv_w, a_log, dt_bias, gdn_norm_g, q_norm_g, w_uq, kv_norm_g,
           w_ukv, w_out, ln1_g, ln1_b, w_gate_up, w_down, ln2_g, ln2_b, w_ple, w_ple_gate):
    B, S, D = x.shape
    depth = w_in.shape[0]
    d_ff = w_down.shape[1]
    alpha = (2.0 * depth) ** 0.25
    tm = _tile_rows(S, 512)
    gdn_rows = _tile_rows(S, 1024)
    ff_chunk = -(-d_ff // (2 * V7X_MXU_COLS)) * V7X_MXU_COLS

    w_in_r, w_uq_r, w_ukv_r, alog_pad, dtb_pad = _reorder_weights(w_in, w_uq, w_ukv, a_log, dt_bias)
    cos_t, sin_t = _rope_tables(positions)
    w_out_b = w_out.astype(BF16)
    w_gu_b = w_gate_up.astype(BF16)
    w_down_b = w_down.astype(BF16)
    w_ple_b = w_ple.astype(BF16)
    w_pg_b = w_ple_gate.astype(BF16)
    vec = lambda a: a[:, None, :]
    p3d = p.reshape(depth, B * S, -1)

    for i in range(depth):
        qg, kg, vg, zg, gbcol, gbrow, qt, ka, vt = _prep_call(
            x, i, w_in_r, conv_w, alog_pad, dtb_pad, vec(q_norm_g), vec(kv_norm_g), w_uq_r, w_ukv_r,
            cos_t, sin_t, tm=tm)
        og = _gdn_call(qg, kg, vg, zg, gbcol, gbrow, i, vec(gdn_norm_g), rows=gdn_rows, group=4)
        om = _attn_call(qt, ka, vt, blk=tm, hps=MLA_HEADS)
        x = _post_call(
            x.reshape(B * S, D), og.reshape(B * S, -1), om.reshape(B * S, -1), p3d, i, w_out_b,
            vec(ln1_g), vec(ln1_b), w_gu_b, w_down_b, vec(ln2_g), vec(ln2_b), w_ple_b, w_pg_b,
            tm=tm, alpha=alpha, ff_chunk=ff_chunk,
        ).reshape(B, S, D)
    return x
```

```python
import functools

import jax
import jax.numpy as jnp
import numpy as np
from jax import lax
from jax.experimental import pallas as pl
from jax.experimental.pallas import tpu as pltpu

F32 = jnp.float32
BF16 = jnp.bfloat16

CHUNK = 64
GDN_HEADS = 4
GDN_DK = 128
GDN_DV = 128
CONV_WIDTH = 4
MLA_HEADS = 4
MLA_NOPE = 128
MLA_ROPE = 64
MLA_V = 128
Q_LORA = 384
KV_LORA = 256
ROPE_THETA = 10000.0
LN_EPS = 1e-5
RMS_EPS = 1e-6

GDN_QK = GDN_HEADS * GDN_DK
GDN_VW = GDN_HEADS * GDN_DV
QKV_W = 2 * GDN_QK + GDN_VW
LANES = 128
Z_OFF = QKV_W
CQ_OFF = Z_OFF + GDN_VW
CKV_OFF = CQ_OFF + Q_LORA
LAST_OFF = CKV_OFF + KV_LORA
IN_PAD = LAST_OFF + LANES
BETA_LANE = MLA_ROPE
G_LANE = MLA_ROPE + GDN_HEADS
ATT_DK = 2 * LANES
VT_ROWS = MLA_V + 16
CONV_COLS = GDN_QK

V7X_VMEM_LIMIT = 56 * 1024 * 1024
V7X_MXU_COLS = 256

HIGHEST = lax.Precision.HIGHEST


def _dot(a, b, precision=None):
    return jnp.dot(a, b, preferred_element_type=F32, precision=precision)


def _dot_nt(a, b, precision=None):
    return lax.dot_general(a, b, (((1,), (1,)), ((), ())), preferred_element_type=F32,
                           precision=precision)


def _dot_tn(a, b, precision=None):
    return lax.dot_general(a, b, (((0,), (0,)), ((), ())), preferred_element_type=F32,
                           precision=precision)


def _bdot(a, b):
    return _dot(a.astype(BF16), b.astype(BF16))


def _bdot_nt(a, b):
    return _dot_nt(a.astype(BF16), b.astype(BF16))


def _bdot_tn(a, b):
    return _dot_tn(a.astype(BF16), b.astype(BF16))


def _sigmoid(x):
    return 1.0 / (1.0 + jnp.exp(-x))


def _silu(x):
    return x * _sigmoid(x)


def _softplus(x):
    return jnp.maximum(x, 0.0) + jnp.log(1.0 + jnp.exp(-jnp.abs(x)))


def _layer_spec(arr, layer, split=None):
    block = list(arr.shape[1:])
    index = [0] * len(block)
    if split is not None:
        axis, parts, which = split
        block[axis] //= parts
        index[axis] = which
    return pl.BlockSpec((None, *block), lambda *_: (layer, *index), pipeline_mode=pl.Buffered(1))


def _prep_kernel(x_ref, w_in_ref, conv_w_ref, alog_ref, dtb_ref, qng_ref, kvng_ref, w_uq_ref,
                 w_ukv_ref, cos_ref, sin_ref,
                 qg_ref, kg_ref, vg_ref, zg_ref, gbcol_ref, gbrow_ref, qt_ref, ka_ref, vt_ref,
                 cbuf, *, tm, att_scale):
    @pl.when(pl.program_id(1) == 0)
    def _():
        cbuf[...] = jnp.zeros_like(cbuf)

    row8 = lax.broadcasted_iota(jnp.int32, (8, CONV_COLS), 0)

    xb = x_ref[0].astype(BF16)
    lane = lax.broadcasted_iota(jnp.int32, (tm, LANES), 1)
    cos = cos_ref[0]
    sin = sin_ref[0]
    first_half = (lane % MLA_ROPE) < (MLA_ROPE // 2)
    nh = MLA_HEADS * MLA_NOPE

    def proj(c0, c1):
        return _dot(xb, w_in_ref[:, c0:c1])

    def rope(blk):
        partner = jnp.where(first_half, pltpu.roll(blk, LANES - MLA_ROPE // 2, 1),
                            pltpu.roll(blk, MLA_ROPE // 2, 1))
        return blk * cos + partner * sin

    def conv_group(grp, part, out_ref):
        c0 = grp * GDN_QK + part * CONV_COLS
        c1 = c0 + CONV_COLS
        o0 = part * CONV_COLS
        pre = proj(c0, c1)
        cw = conv_w_ref[:, c0:c1]
        y = pre * cw[0:1, :]
        for j in range(1, CONV_WIDTH):
            carry_in = cbuf[j - 1:j, c0:c1]
            cbuf[j - 1:j, c0:c1] = y[tm - 1:tm, :]
            shifted = pltpu.roll(y, 1, 0)
            top = jnp.where(row8 == 0, carry_in, shifted[0:8, :])
            y = jnp.concatenate([top, shifted[8:, :]], 0) + pre * cw[j:j + 1, :]
        y = _silu(y)
        if out_ref is vg_ref:
            out_ref[0, :, o0:o0 + CONV_COLS] = y.astype(BF16)
            return
        out_scale = GDN_DK ** -0.5 if out_ref is qg_ref else 1.0
        for lo in range(0, CONV_COLS, GDN_DK):
            yh = y[:, lo:lo + GDN_DK]
            yn = yh * (lax.rsqrt(jnp.sum(yh * yh, -1, keepdims=True) + RMS_EPS) * out_scale)
            out_ref[0, :, o0 + lo:o0 + lo + GDN_DK] = yn.astype(BF16)

    for grp, out_ref in enumerate((qg_ref, kg_ref, vg_ref)):
        for part in range(GDN_QK // CONV_COLS):
            conv_group(grp, part, out_ref)

    latent = proj(CQ_OFF, IN_PAD)
    cq = latent[:, :Q_LORA]
    cqn = cq * lax.rsqrt(jnp.mean(cq * cq, -1, keepdims=True) + RMS_EPS) * qng_ref[...]
    qm = _dot(cqn.astype(BF16), w_uq_ref[...])
    for hd in range(MLA_HEADS):
        lo = hd * LANES
        q_rope = qm[:, nh + lo:nh + lo + LANES] * cos + qm[:, 2 * nh + lo:2 * nh + lo + LANES] * sin
        q_full = jnp.concatenate([qm[:, lo:lo + LANES], q_rope], -1) * att_scale
        qt_ref[0, hd, 0] = q_full.T.astype(BF16)

    ckv = latent[:, Q_LORA:Q_LORA + KV_LORA]
    last = latent[:, Q_LORA + KV_LORA:]
    ckvn = ckv * lax.rsqrt(jnp.mean(ckv * ckv, -1, keepdims=True) + RMS_EPS) * kvng_ref[...]
    kvm = _dot(ckvn.astype(BF16), w_ukv_ref[...])
    k_rope = jnp.where(lane < MLA_ROPE, rope(last), 0.0)
    ones_rows = (lax.broadcasted_iota(jnp.int32, (VT_ROWS - MLA_V, tm), 0) == 0).astype(BF16)
    for hd in range(MLA_HEADS):
        lo = hd * LANES
        ka_ref[0, hd] = jnp.concatenate([kvm[:, lo:lo + LANES], k_rope], -1).astype(BF16)
        vt_ref[0, hd, 0, 0:MLA_V, :] = kvm[:, nh + lo:nh + lo + LANES].T.astype(BF16)
        vt_ref[0, hd, 0, MLA_V:VT_ROWS, :] = ones_rows

    beta = _sigmoid(last)
    g = -jnp.exp(alog_ref[...]) * _softplus(last + dtb_ref[...])
    gb = jnp.where((lane >= BETA_LANE) & (lane < G_LANE), beta,
                   jnp.where((lane >= G_LANE) & (lane < G_LANE + GDN_HEADS), g, 0.0))
    gbcol_ref[0] = gb
    gbrow_ref[0] = gb.T[BETA_LANE:BETA_LANE + 2 * GDN_HEADS, :]

    zg_ref[0] = proj(Z_OFF, Z_OFF + GDN_VW).astype(BF16)


def _prep_call(x, layer, w_in_r, conv_w, alog_pad, dtb_pad, qn_g, kvn_g, w_uq_r, w_ukv_r, cos_t,
               sin_t, *, tm):
    B, S, D = x.shape
    params = (w_in_r, conv_w, alog_pad, dtb_pad, qn_g, kvn_g, w_uq_r, w_ukv_r)
    nt = S // tm
    row = lambda w: pl.BlockSpec((1, tm, w), lambda b, i: (b, i, 0))
    head = pl.BlockSpec((1, MLA_HEADS, tm, ATT_DK), lambda b, i: (b, 0, i, 0))
    out_shape = (
        jax.ShapeDtypeStruct((B, S, GDN_QK), BF16), jax.ShapeDtypeStruct((B, S, GDN_QK), BF16),
        jax.ShapeDtypeStruct((B, S, GDN_VW), BF16), jax.ShapeDtypeStruct((B, S, GDN_VW), BF16),
        jax.ShapeDtypeStruct((B, S, LANES), F32), jax.ShapeDtypeStruct((B, 2 * GDN_HEADS, S), F32),
        jax.ShapeDtypeStruct((B, MLA_HEADS, nt, ATT_DK, tm), BF16),
        jax.ShapeDtypeStruct((B, MLA_HEADS, S, ATT_DK), BF16),
        jax.ShapeDtypeStruct((B, MLA_HEADS, nt, VT_ROWS, tm), BF16),
    )
    transposed = lambda r: pl.BlockSpec((1, MLA_HEADS, 1, r, tm), lambda b, i: (b, 0, i, 0, 0))
    out_specs = (
        row(GDN_QK), row(GDN_QK), row(GDN_VW), row(GDN_VW), row(LANES),
        pl.BlockSpec((1, 2 * GDN_HEADS, tm), lambda b, i: (b, 0, i)),
        transposed(ATT_DK), head, transposed(VT_ROWS),
    )
    in_specs = [row(D)] + [_layer_spec(a, layer) for a in params] + [row(LANES), row(LANES)]
    kern = functools.partial(_prep_kernel, tm=tm,
                             att_scale=(MLA_NOPE + MLA_ROPE) ** -0.5 * float(np.log2(np.e)))
    return pl.pallas_call(
        kern, out_shape=out_shape, grid=(B, nt), in_specs=in_specs, out_specs=out_specs,
        scratch_shapes=[pltpu.VMEM((8, QKV_W), F32)],
        compiler_params=pltpu.CompilerParams(dimension_semantics=("arbitrary", "arbitrary"),
                                             vmem_limit_bytes=V7X_VMEM_LIMIT),
        name="prep",
    )(x, *params, cos_t, sin_t)


def _gdn_kernel(q_ref, k_ref, v_ref, z_ref, gbcol_ref, gbrow_ref, ng_ref, o_ref, state, *, nchunk,
                group):
    @pl.when(pl.program_id(0) == 0)
    def _():
        state[...] = jnp.zeros_like(state)

    nbatch = q_ref.shape[0]
    C = CHUNK
    W = 2 * C
    row = lax.broadcasted_iota(jnp.int32, (C, W), 0)
    lane = lax.broadcasted_iota(jnp.int32, (C, W), 1)
    left = lane < C
    col = lane % C
    tri_incl = row >= col
    tri_strict = row > col
    eye = (row == col).astype(F32)
    left_row = left[0:1, :]
    grows = group * C
    rr = lax.broadcasted_iota(jnp.int32, (grows, grows), 0)
    cc = lax.broadcasted_iota(jnp.int32, (grows, grows), 1)
    cum_down = (((rr // C) == (cc // C)) & (rr >= cc)).astype(F32)
    rr2 = lax.broadcasted_iota(jnp.int32, (grows, 2 * grows), 0)
    cc2 = lax.broadcasted_iota(jnp.int32, (grows, 2 * grows), 1)
    cum_right = (((rr2 // C) == (cc2 // W)) & (rr2 <= (cc2 // W) * C + cc2 % C)).astype(F32)
    ng = ng_ref[...]
    zeros_c = jnp.zeros((C, GDN_DK), F32)
    zeros_s = jnp.zeros((GDN_DK, GDN_DV), F32)
    pairs = range(GDN_HEADS // 2)
    batches = range(nbatch)

    def wide(xa, xb):
        return jnp.concatenate([jnp.broadcast_to(xa, (C, GDN_DK)), jnp.broadcast_to(xb, (C, GDN_DK))], 1)

    def block_diag(x):
        return jnp.concatenate([jnp.where(left, x, 0.0), jnp.where(left, 0.0, x)], 0)

    def prepare(chunks, out):
        g0 = chunks[0] * C
        gcum_c, gcum_r, gb_grp = [], [], []
        for b in batches:
            gb_grp.append(gbcol_ref[b, g0:g0 + grows, :])
            gcum_c.append(_dot(cum_down, gb_grp[b], HIGHEST))
            gcum_r.append(_dot(gbrow_ref[b, :, g0:g0 + grows], cum_right, HIGHEST))
        low, attn, rhs, qdec, kdec, egl = [], [], [], [], [], []
        for ci, b, pr in [(ci, b, pr) for ci in chunks for b in batches for pr in pairs]:
            r0, n0, lo = ci * C, ci * C - g0, pr * 2 * GDN_DK
            ha, hb = 2 * pr, 2 * pr + 1
            gc_a = gcum_c[b][n0:n0 + C, G_LANE + ha:G_LANE + ha + 1]
            gc_b = gcum_c[b][n0:n0 + C, G_LANE + hb:G_LANE + hb + 1]
            gr_w = gcum_r[b][:, 2 * n0:2 * n0 + W]
            gr = jnp.where(left_row, gr_w[GDN_HEADS + ha:GDN_HEADS + ha + 1, :],
                           gr_w[GDN_HEADS + hb:GDN_HEADS + hb + 1, :])
            gl_a, gl_b = gc_a[C - 1:C, :], gc_b[C - 1:C, :]
            beta = wide(gb_grp[b][n0:n0 + C, BETA_LANE + ha:BETA_LANE + ha + 1],
                        gb_grp[b][n0:n0 + C, BETA_LANE + hb:BETA_LANE + hb + 1])
            decay = jnp.exp(jnp.where(tri_incl, jnp.where(left, gc_a, gc_b) - gr, -jnp.inf))
            eg = wide(jnp.exp(gc_a), jnp.exp(gc_b))
            qp = q_ref[b, r0:r0 + C, lo:lo + 2 * GDN_DK].astype(F32)
            kp = k_ref[b, r0:r0 + C, lo:lo + 2 * GDN_DK].astype(F32)
            vp = v_ref[b, r0:r0 + C, lo:lo + 2 * GDN_DV].astype(F32)
            kb = kp * beta
            k_bd = jnp.concatenate([jnp.concatenate([kp[:, :GDN_DK], zeros_c], 1),
                                    jnp.concatenate([zeros_c, kp[:, GDN_DK:]], 1)], 0)
            kq = _bdot_nt(jnp.concatenate([kb, qp], 0), k_bd)
            low.append(jnp.where(tri_strict, kq[:C] * decay, 0.0))
            attn.append(jnp.where(tri_incl, kq[C:] * decay, 0.0))
            vb, kbeg = vp * beta, kb * eg
            rhs.append(jnp.concatenate(
                [jnp.concatenate([vb[:, :GDN_DV], kbeg[:, :GDN_DK], zeros_c, zeros_c], 1),
                 jnp.concatenate([zeros_c, zeros_c, vb[:, GDN_DV:], kbeg[:, GDN_DK:]], 1)], 0))
            qdec.append(qp * eg)
            kdec.append(kp * wide(jnp.exp(gl_a - gc_a), jnp.exp(gl_b - gc_b)))
            egl.append((jnp.exp(gl_a), jnp.exp(gl_b)))
        yield
        items = range(len(low))
        inv = [eye - low[it] for it in items]
        pw = [_bdot(low[it], block_diag(low[it])) for it in items]
        yield
        span = 2
        while span < C:
            if 2 * span >= C:
                inv = [inv[it] + _bdot(inv[it], block_diag(pw[it])) for it in items]
            else:
                both = [_bdot(jnp.concatenate([inv[it], pw[it]], 0), block_diag(pw[it]))
                        for it in items]
                inv = [inv[it] + both[it][:C] for it in items]
                pw = [both[it][C:] for it in items]
            span *= 2
            yield
        sol = [_bdot(inv[it], rhs[it]) for it in items]
        per_chunk = nbatch * len(pairs)
        for n, ci in enumerate(chunks):
            sl = slice(n * per_chunk, (n + 1) * per_chunk)
            out[ci] = dict(
                u=[jnp.concatenate([x[:, :GDN_DV], x[:, 2 * GDN_DV:3 * GDN_DV]], 1) for x in sol[sl]],
                w=[jnp.concatenate([x[:, GDN_DV:2 * GDN_DV], x[:, 3 * GDN_DV:]], 1) for x in sol[sl]],
                attn=attn[sl], qdec=qdec[sl], kdec=kdec[sl], egl=egl[sl])
        yield

    def recur_chunk(ci, pre, s):
        r0 = ci * C
        items = [(b, pr) for b in batches for pr in pairs]
        s_bd = [jnp.concatenate([jnp.concatenate([s[b][2 * pr], zeros_s], 1),
                                 jnp.concatenate([zeros_s, s[b][2 * pr + 1]], 1)], 0)
                for b, pr in items]
        ws = [_bdot(jnp.concatenate([pre["w"][n], pre["qdec"][n]], 0), s_bd[n])
              for n in range(len(items))]
        yield
        v_new = [pre["u"][n] - ws[n][:C] for n in range(len(items))]
        v_bd = [jnp.concatenate([jnp.concatenate([v[:, :GDN_DV], zeros_c], 1),
                                 jnp.concatenate([zeros_c, v[:, GDN_DV:]], 1)], 0) for v in v_new]
        o = [ws[n][C:] + _bdot(pre["attn"][n], v_bd[n]) for n in range(len(items))]
        for n, (b, pr) in enumerate(items):
            for half in range(2):
                lo = half * GDN_DK
                s[b][2 * pr + half] = s[b][2 * pr + half] * pre["egl"][n][half] + _bdot_tn(
                    pre["kdec"][n][:, lo:lo + GDN_DK], v_new[n][:, lo:lo + GDN_DV])
        yield
        for n, (b, pr) in enumerate(items):
            for half in range(2):
                hd = 2 * pr + half
                oh = o[n][:, half * GDN_DV:(half + 1) * GDN_DV]
                zh = z_ref[b, r0:r0 + C, hd * GDN_DV:(hd + 1) * GDN_DV].astype(F32)
                on = oh * lax.rsqrt(jnp.mean(oh * oh, -1, keepdims=True) + RMS_EPS) * ng
                o_ref[b, r0:r0 + C, hd * GDN_DV:(hd + 1) * GDN_DV] = (on * _silu(zh)).astype(BF16)
        yield

    def recur(chunks, prepared, s):
        for ci in chunks:
            yield from recur_chunk(ci, prepared[ci], s)

    s = [[state[b, hd] for hd in range(GDN_HEADS)] for b in batches]
    groups = [list(range(g, min(g + group, nchunk))) for g in range(0, nchunk, group)]
    prepared = {}
    for _ in prepare(groups[0], prepared):
        pass
    for gi, chunks in enumerate(groups):
        nxt = prepare(groups[gi + 1], prepared) if gi + 1 < len(groups) else iter(())
        cur = recur(chunks, prepared, s)
        nxt_done = cur_done = False
        while not (nxt_done and cur_done):
            nxt_done = nxt_done or next(nxt, "end") == "end"
            cur_done = cur_done or next(cur, "end") == "end"
    for b in batches:
        for hd in range(GDN_HEADS):
            state[b, hd] = s[b][hd]


def _gdn_call(qg, kg, vg, zg, gbcol, gbrow, layer, norm_g, *, rows, group):
    B, S, _ = qg.shape
    row = lambda w: pl.BlockSpec((B, rows, w), lambda i: (0, i, 0))
    kern = functools.partial(_gdn_kernel, nchunk=rows // CHUNK, group=group)
    return pl.pallas_call(
        kern, out_shape=jax.ShapeDtypeStruct((B, S, GDN_VW), BF16), grid=(S // rows,),
        in_specs=[row(GDN_QK), row(GDN_QK), row(GDN_VW), row(GDN_VW), row(LANES),
                  pl.BlockSpec((B, 2 * GDN_HEADS, rows), lambda i: (0, 0, i)),
                  _layer_spec(norm_g, layer)],
        out_specs=row(GDN_VW),
        scratch_shapes=[pltpu.VMEM((B, GDN_HEADS, GDN_DK, GDN_DV), F32)],
        compiler_params=pltpu.CompilerParams(dimension_semantics=("arbitrary",),
                                             vmem_limit_bytes=V7X_VMEM_LIMIT),
        name="gdn",
    )(qg, kg, vg, zg, gbcol, gbrow, norm_g)


def _attn_kernel(qt_ref, k_ref, vt_ref, o_ref, m_ref, acc_ref, st_ref, *, blk, hps):
    i = pl.program_id(2)
    m_ref[...] = jnp.full_like(m_ref, -jnp.inf)
    acc_ref[...] = jnp.zeros_like(acc_ref)

    def scores(hd, j, masked=False):
        start = pl.multiple_of(j * blk, blk)
        s = _dot(k_ref[0, hd, pl.ds(start, blk), :], qt_ref[0, hd, 0]).astype(BF16)
        if masked:
            kc = lax.broadcasted_iota(jnp.int32, (blk, blk), 0) // CHUNK
            qc = lax.broadcasted_iota(jnp.int32, (blk, blk), 1) // CHUNK
            s = jnp.where(kc <= qc, s, -jnp.inf)
        st_ref[hd] = s

    def process(j, nxt):
        for hd in range(hps):
            s = st_ref[hd]
            m_old = m_ref[hd]
            m_new = jnp.maximum(m_old, jnp.max(s, axis=0, keepdims=True).astype(F32))
            alpha = jnp.exp2(m_old - m_new)
            p = jnp.exp2(s - m_new.astype(BF16))
            acc_ref[hd] = alpha * acc_ref[hd] + _dot(vt_ref[0, hd, j], p)
            m_ref[hd] = m_new
            if nxt is not None:
                scores(hd, nxt)

    for hd in range(hps):
        scores(hd, i, masked=True)
    process(i, 0)

    @pl.when(i > 0)
    def _():
        lax.fori_loop(0, i - 1, lambda j, carry: (process(j, j + 1), carry)[1], 0)
        process(i - 1, None)

    for hd in range(hps):
        out = acc_ref[hd, 0:MLA_V, :] / acc_ref[hd, MLA_V:MLA_V + 1, :]
        o_ref[0, :, hd * MLA_V:(hd + 1) * MLA_V] = out.T.astype(BF16)


def _attn_call(qt, ka, vt, *, blk, hps):
    B, H, S, _ = ka.shape
    nb = S // blk
    kern = functools.partial(_attn_kernel, blk=blk, hps=hps)
    return pl.pallas_call(
        kern, out_shape=jax.ShapeDtypeStruct((B, S, H * MLA_V), BF16), grid=(B, H // hps, nb),
        in_specs=[pl.BlockSpec((1, hps, 1, ATT_DK, blk), lambda b, h, i: (b, h, i, 0, 0)),
                  pl.BlockSpec((1, hps, S, ATT_DK), lambda b, h, i: (b, h, 0, 0),
                               pipeline_mode=pl.Buffered(1)),
                  pl.BlockSpec((1, hps, nb, VT_ROWS, blk), lambda b, h, i: (b, h, 0, 0, 0),
                               pipeline_mode=pl.Buffered(1))],
        out_specs=pl.BlockSpec((1, blk, hps * MLA_V), lambda b, h, i: (b, i, h)),
        scratch_shapes=[pltpu.VMEM((hps, 1, blk), F32), pltpu.VMEM((hps, VT_ROWS, blk), F32),
                        pltpu.VMEM((hps, blk, blk), BF16)],
        compiler_params=pltpu.CompilerParams(
            dimension_semantics=("arbitrary", "arbitrary", "arbitrary"),
            vmem_limit_bytes=V7X_VMEM_LIMIT),
        name="attn",
    )(qt, ka, vt)


def _layer_norm(x, g, b):
    mu = jnp.mean(x, -1, keepdims=True)
    xc = x - mu
    var = jnp.mean(xc * xc, -1, keepdims=True)
    return xc * lax.rsqrt(var + LN_EPS) * g + b


def _post_kernel(x_ref, og_ref, om_ref, p_ref, wog_ref, wom_ref, l1g_ref, l1b_ref, wg_ref, wu_ref,
                 wd_ref, l2g_ref, l2b_ref, wple_ref, wpg_ref, o_ref, *, alpha, ff_chunk):
    mix = _dot(og_ref[...], wog_ref[...]) + _dot(om_ref[...], wom_ref[...])
    x1 = _layer_norm(alpha * x_ref[...] + mix, l1g_ref[...], l1b_ref[...])
    x1b = x1.astype(BF16)
    d_ff = wd_ref.shape[0]
    ff = jnp.zeros_like(x1)
    for c0 in range(0, d_ff, ff_chunk):
        c1 = min(c0 + ff_chunk, d_ff)
        gate = _dot(x1b, wg_ref[:, c0:c1])
        up = _dot(x1b, wu_ref[:, c0:c1])
        ff = ff + _dot((_silu(gate) * up).astype(BF16), wd_ref[c0:c1, :])
    x2 = _layer_norm(alpha * x1 + ff, l2g_ref[...], l2b_ref[...])
    emb = _dot(p_ref[...].astype(BF16), wple_ref[...])
    o_ref[...] = x2 + _sigmoid(_dot(x2.astype(BF16), wpg_ref[...])) * emb


def _post_call(x2d, og, om, p3d, layer, w_out, ln1_g, ln1_b, w_gate_up, w_down, ln2_g, ln2_b, w_ple,
               w_pg, *, tm, alpha, ff_chunk):
    T, D = x2d.shape
    row = lambda w: pl.BlockSpec((tm, w), lambda i: (i, 0))
    weights = ((w_out, (0, 2, 0)), (w_out, (0, 2, 1)), (ln1_g, None), (ln1_b, None),
               (w_gate_up, (1, 2, 0)), (w_gate_up, (1, 2, 1)), (w_down, None), (ln2_g, None),
               (ln2_b, None), (w_ple, None), (w_pg, None))
    kern = functools.partial(_post_kernel, alpha=alpha, ff_chunk=ff_chunk)
    return pl.pallas_call(
        kern, out_shape=jax.ShapeDtypeStruct((T, D), F32), grid=(T // tm,),
        in_specs=[row(D), row(og.shape[1]), row(om.shape[1]),
                  pl.BlockSpec((None, tm, p3d.shape[2]), lambda i: (layer, i, 0))]
        + [_layer_spec(w, layer, split) for w, split in weights],
        out_specs=row(D),
        compiler_params=pltpu.CompilerParams(dimension_semantics=("arbitrary",),
                                             vmem_limit_bytes=V7X_VMEM_LIMIT),
        name="post",
    )(x2d, og, om, p3d, *[w for w, _ in weights])


def _reorder_weights(w_in, w_uq, w_ukv, a_log, dt_bias):
    depth, d, _ = w_in.shape
    q, k, v, z, b, a, cq, ckv, kr = jnp.split(
        w_in, np.cumsum((GDN_QK, GDN_QK, GDN_VW, GDN_VW, GDN_HEADS, GDN_HEADS, Q_LORA, KV_LORA))
        .tolist(), axis=-1)
    pad = jnp.zeros((depth, d, LANES - MLA_ROPE - 2 * GDN_HEADS), w_in.dtype)
    w_in_r = jnp.concatenate([q, k, v, z, cq, ckv, kr, b, a, pad], -1).astype(BF16)

    uq = w_uq.reshape(depth, Q_LORA, MLA_HEADS, MLA_NOPE + MLA_ROPE)
    uq_nope = uq[..., :MLA_NOPE].reshape(depth, Q_LORA, MLA_HEADS * MLA_NOPE)
    lane_pad = ((0, 0), (0, 0), (0, 0), (0, LANES - MLA_ROPE))
    rope_cols = uq[..., MLA_NOPE:]
    partner_cols = jnp.concatenate([rope_cols[..., MLA_ROPE // 2:], rope_cols[..., :MLA_ROPE // 2]], -1)
    w_uq_r = jnp.concatenate(
        [uq_nope, jnp.pad(rope_cols, lane_pad).reshape(depth, Q_LORA, MLA_HEADS * LANES),
         jnp.pad(partner_cols, lane_pad).reshape(depth, Q_LORA, MLA_HEADS * LANES)], -1).astype(BF16)

    ukv = w_ukv.reshape(depth, KV_LORA, MLA_HEADS, MLA_NOPE + MLA_V)
    w_ukv_r = jnp.concatenate([ukv[..., :MLA_NOPE].reshape(depth, KV_LORA, -1),
                               ukv[..., MLA_NOPE:].reshape(depth, KV_LORA, -1)], -1).astype(BF16)

    lanes_pad = ((0, 0), (0, 0), (G_LANE, LANES - G_LANE - GDN_HEADS))
    alog_pad = jnp.pad(a_log[:, None, :], lanes_pad)
    dtb_pad = jnp.pad(dt_bias[:, None, :], lanes_pad)
    return w_in_r, w_uq_r, w_ukv_r, alog_pad, dtb_pad


def _rope_tables(positions):
    inv_freq = ROPE_THETA ** (-jnp.arange(0, MLA_ROPE, 2, dtype=F32) / MLA_ROPE)
    ang = positions.astype(F32)[..., None] * inv_freq
    cos, sin = jnp.cos(ang), jnp.sin(ang)
    reps = LANES // MLA_ROPE
    return (jnp.tile(jnp.concatenate([cos, cos], -1), (1, 1, reps)),
            jnp.tile(jnp.concatenate([-sin, sin], -1), (1, 1, reps)))


def _tile_rows(s, want):
    t = min(want, s)
    while s % t:
        t //= 2
    return t


def kernel(x, p, positions, w_in, conv_w, a_log, dt_bias, gdn_norm_g, q_norm_g, w_uq, kv_norm_g,
           w_ukv, w_out, ln1_g, ln1_b, w_gate_up, w_down, ln2_g, ln2_b, w_ple, w_ple_gate):
    B, S, D = x.shape
    depth = w_in.shape[0]
    d_ff = w_down.shape[1]
    alpha = (2.0 * depth) ** 0.25
    tm = _tile_rows(S, 512)
    gdn_rows = _tile_rows(S, 256)
    ff_chunk = -(-d_ff // (2 * V7X_MXU_COLS)) * V7X_MXU_COLS

    w_in_r, w_uq_r, w_ukv_r, alog_pad, dtb_pad = _reorder_weights(w_in, w_uq, w_ukv, a_log, dt_bias)
    cos_t, sin_t = _rope_tables(positions)
    w_out_b = w_out.astype(BF16)
    w_gu_b = w_gate_up.astype(BF16)
    w_down_b = w_down.astype(BF16)
    w_ple_b = w_ple.astype(BF16)
    w_pg_b = w_ple_gate.astype(BF16)
    vec = lambda a: a[:, None, :]
    p3d = p.reshape(depth, B * S, -1)

    for i in range(depth):
        qg, kg, vg, zg, gbcol, gbrow, qt, ka, vt = _prep_call(
            x, i, w_in_r, conv_w, alog_pad, dtb_pad, vec(q_norm_g), vec(kv_norm_g), w_uq_r, w_ukv_r,
            cos_t, sin_t, tm=tm)
        og = _gdn_call(qg, kg, vg, zg, gbcol, gbrow, i, vec(gdn_norm_g), rows=gdn_rows, group=2)
        om = _attn_call(qt, ka, vt, blk=tm, hps=MLA_HEADS)
        x = _post_call(
            x.reshape(B * S, D), og.reshape(B * S, -1), om.reshape(B * S, -1), p3d, i, w_out_b,
            vec(ln1_g), vec(ln1_b), w_gu_b, w_down_b, vec(ln2_g), vec(ln2_b), w_ple_b, w_pg_b,
            tm=tm, alpha=alpha, ff_chunk=ff_chunk,
        ).reshape(B, S, D)
    return x
```

```python
import functools

import jax
import jax.numpy as jnp
import numpy as np
from jax import lax
from jax.experimental import pallas as pl
from jax.experimental.pallas import tpu as pltpu

F32 = jnp.float32
BF16 = jnp.bfloat16

CHUNK = 64
GDN_HEADS = 4
GDN_DK = 128
GDN_DV = 128
CONV_WIDTH = 4
MLA_HEADS = 4
MLA_NOPE = 128
MLA_ROPE = 64
MLA_V = 128
Q_LORA = 384
KV_LORA = 256
ROPE_THETA = 10000.0
LN_EPS = 1e-5
RMS_EPS = 1e-6

GDN_QK = GDN_HEADS * GDN_DK
GDN_VW = GDN_HEADS * GDN_DV
QKV_W = 2 * GDN_QK + GDN_VW
LANES = 128
Z_OFF = QKV_W
CQ_OFF = Z_OFF + GDN_VW
CKV_OFF = CQ_OFF + Q_LORA
LAST_OFF = CKV_OFF + KV_LORA
IN_PAD = LAST_OFF + LANES
BETA_LANE = MLA_ROPE
G_LANE = MLA_ROPE + GDN_HEADS
ATT_DK = 2 * LANES
VT_ROWS = MLA_V + 16
CONV_COLS = GDN_QK

V7X_VMEM_LIMIT = 56 * 1024 * 1024
V7X_MXU_COLS = 256

HIGHEST = lax.Precision.HIGHEST


def _dot(a, b, precision=None):
    return jnp.dot(a, b, preferred_element_type=F32, precision=precision)


def _dot_nt(a, b, precision=None):
    return lax.dot_general(a, b, (((1,), (1,)), ((), ())), preferred_element_type=F32,
                           precision=precision)


def _dot_tn(a, b, precision=None):
    return lax.dot_general(a, b, (((0,), (0,)), ((), ())), preferred_element_type=F32,
                           precision=precision)


def _bdot(a, b):
    return _dot(a.astype(BF16), b.astype(BF16))


def _bdot_nt(a, b):
    return _dot_nt(a.astype(BF16), b.astype(BF16))


def _bdot_tn(a, b):
    return _dot_tn(a.astype(BF16), b.astype(BF16))


def _sigmoid(x):
    return 1.0 / (1.0 + jnp.exp(-x))


def _silu(x):
    return x * _sigmoid(x)


def _softplus(x):
    return jnp.maximum(x, 0.0) + jnp.log(1.0 + jnp.exp(-jnp.abs(x)))


def _layer_spec(arr, layer, split=None):
    block = list(arr.shape[1:])
    index = [0] * len(block)
    if split is not None:
        axis, parts, which = split
        block[axis] //= parts
        index[axis] = which
    return pl.BlockSpec((None, *block), lambda *_: (layer, *index), pipeline_mode=pl.Buffered(1))


def _prep_kernel(x_ref, w_in_ref, conv_w_ref, alog_ref, dtb_ref, qng_ref, kvng_ref, w_uq_ref,
                 w_ukv_ref, cos_ref, sin_ref,
                 qg_ref, kg_ref, vg_ref, zg_ref, gbcol_ref, gbrow_ref, qt_ref, ka_ref, vt_ref,
                 cbuf, *, tm, att_scale):
    @pl.when(pl.program_id(1) == 0)
    def _():
        cbuf[...] = jnp.zeros_like(cbuf)

    row8 = lax.broadcasted_iota(jnp.int32, (8, CONV_COLS), 0)

    xb = x_ref[0].astype(BF16)
    lane = lax.broadcasted_iota(jnp.int32, (tm, LANES), 1)
    cos = cos_ref[0]
    sin = sin_ref[0]
    first_half = (lane % MLA_ROPE) < (MLA_ROPE // 2)
    nh = MLA_HEADS * MLA_NOPE

    def proj(c0, c1):
        return _dot(xb, w_in_ref[:, c0:c1])

    def rope(blk):
        partner = jnp.where(first_half, pltpu.roll(blk, LANES - MLA_ROPE // 2, 1),
                            pltpu.roll(blk, MLA_ROPE // 2, 1))
        return blk * cos + partner * sin

    def conv_group(grp, part, out_ref):
        c0 = grp * GDN_QK + part * CONV_COLS
        c1 = c0 + CONV_COLS
        o0 = part * CONV_COLS
        pre = proj(c0, c1)
        cw = conv_w_ref[:, c0:c1]
        y = pre * cw[0:1, :]
        for j in range(1, CONV_WIDTH):
            carry_in = cbuf[j - 1:j, c0:c1]
            cbuf[j - 1:j, c0:c1] = y[tm - 1:tm, :]
            shifted = pltpu.roll(y, 1, 0)
            top = jnp.where(row8 == 0, carry_in, shifted[0:8, :])
            y = jnp.concatenate([top, shifted[8:, :]], 0) + pre * cw[j:j + 1, :]
        y = _silu(y)
        if out_ref is vg_ref:
            out_ref[0, :, o0:o0 + CONV_COLS] = y.astype(BF16)
            return
        out_scale = GDN_DK ** -0.5 if out_ref is qg_ref else 1.0
        for lo in range(0, CONV_COLS, GDN_DK):
            yh = y[:, lo:lo + GDN_DK]
            yn = yh * (lax.rsqrt(jnp.sum(yh * yh, -1, keepdims=True) + RMS_EPS) * out_scale)
            out_ref[0, :, o0 + lo:o0 + lo + GDN_DK] = yn.astype(BF16)

    for grp, out_ref in enumerate((qg_ref, kg_ref, vg_ref)):
        for part in range(GDN_QK // CONV_COLS):
            conv_group(grp, part, out_ref)

    latent = proj(CQ_OFF, IN_PAD)
    cq = latent[:, :Q_LORA]
    cqn = cq * lax.rsqrt(jnp.mean(cq * cq, -1, keepdims=True) + RMS_EPS) * qng_ref[...]
    qm = _dot(cqn.astype(BF16), w_uq_ref[...])
    for hd in range(MLA_HEADS):
        lo = hd * LANES
        q_rope = qm[:, nh + lo:nh + lo + LANES] * cos + qm[:, 2 * nh + lo:2 * nh + lo + LANES] * sin
        q_full = jnp.concatenate([qm[:, lo:lo + LANES], q_rope], -1) * att_scale
        qt_ref[0, hd, 0] = q_full.T.astype(BF16)

    ckv = latent[:, Q_LORA:Q_LORA + KV_LORA]
    last = latent[:, Q_LORA + KV_LORA:]
    ckvn = ckv * lax.rsqrt(jnp.mean(ckv * ckv, -1, keepdims=True) + RMS_EPS) * kvng_ref[...]
    kvm = _dot(ckvn.astype(BF16), w_ukv_ref[...])
    k_rope = jnp.where(lane < MLA_ROPE, rope(last), 0.0)
    ones_rows = (lax.broadcasted_iota(jnp.int32, (VT_ROWS - MLA_V, tm), 0) == 0).astype(BF16)
    for hd in range(MLA_HEADS):
        lo = hd * LANES
        ka_ref[0, hd] = jnp.concatenate([kvm[:, lo:lo + LANES], k_rope], -1).astype(BF16)
        vt_ref[0, hd, 0, 0:MLA_V, :] = kvm[:, nh + lo:nh + lo + LANES].T.astype(BF16)
        vt_ref[0, hd, 0, MLA_V:VT_ROWS, :] = ones_rows

    beta = _sigmoid(last)
    g = -jnp.exp(alog_ref[...]) * _softplus(last + dtb_ref[...])
    gb = jnp.where((lane >= BETA_LANE) & (lane < G_LANE), beta,
                   jnp.where((lane >= G_LANE) & (lane < G_LANE + GDN_HEADS), g, 0.0))
    gbcol_ref[0] = gb
    gbrow_ref[0] = gb.T[BETA_LANE:BETA_LANE + 2 * GDN_HEADS, :]

    zg_ref[0] = proj(Z_OFF, Z_OFF + GDN_VW).astype(BF16)


def _prep_call(x, layer, w_in_r, conv_w, alog_pad, dtb_pad, qn_g, kvn_g, w_uq_r, w_ukv_r, cos_t,
               sin_t, *, tm):
    B, S, D = x.shape
    params = (w_in_r, conv_w, alog_pad, dtb_pad, qn_g, kvn_g, w_uq_r, w_ukv_r)
    nt = S // tm
    row = lambda w: pl.BlockSpec((1, tm, w), lambda b, i: (b, i, 0))
    head = pl.BlockSpec((1, MLA_HEADS, tm, ATT_DK), lambda b, i: (b, 0, i, 0))
    out_shape = (
        jax.ShapeDtypeStruct((B, S, GDN_QK), BF16), jax.ShapeDtypeStruct((B, S, GDN_QK), BF16),
        jax.ShapeDtypeStruct((B, S, GDN_VW), BF16), jax.ShapeDtypeStruct((B, S, GDN_VW), BF16),
        jax.ShapeDtypeStruct((B, S, LANES), F32), jax.ShapeDtypeStruct((B, 2 * GDN_HEADS, S), F32),
        jax.ShapeDtypeStruct((B, MLA_HEADS, nt, ATT_DK, tm), BF16),
        jax.ShapeDtypeStruct((B, MLA_HEADS, S, ATT_DK), BF16),
        jax.ShapeDtypeStruct((B, MLA_HEADS, nt, VT_ROWS, tm), BF16),
    )
    transposed = lambda r: pl.BlockSpec((1, MLA_HEADS, 1, r, tm), lambda b, i: (b, 0, i, 0, 0))
    out_specs = (
        row(GDN_QK), row(GDN_QK), row(GDN_VW), row(GDN_VW), row(LANES),
        pl.BlockSpec((1, 2 * GDN_HEADS, tm), lambda b, i: (b, 0, i)),
        transposed(ATT_DK), head, transposed(VT_ROWS),
    )
    in_specs = [row(D)] + [_layer_spec(a, layer) for a in params] + [row(LANES), row(LANES)]
    kern = functools.partial(_prep_kernel, tm=tm,
                             att_scale=(MLA_NOPE + MLA_ROPE) ** -0.5 * float(np.log2(np.e)))
    return pl.pallas_call(
        kern, out_shape=out_shape, grid=(B, nt), in_specs=in_specs, out_specs=out_specs,
        scratch_shapes=[pltpu.VMEM((8, QKV_W), F32)],
        compiler_params=pltpu.CompilerParams(dimension_semantics=("arbitrary", "arbitrary"),
                                             vmem_limit_bytes=V7X_VMEM_LIMIT),
        name="prep",
    )(x, *params, cos_t, sin_t)


def _gdn_kernel(q_ref, k_ref, v_ref, z_ref, gbcol_ref, gbrow_ref, ng_ref, o_ref, state, *, nchunk,
                group):
    @pl.when(pl.program_id(0) == 0)
    def _():
        state[...] = jnp.zeros_like(state)

    nbatch = q_ref.shape[0]
    C = CHUNK
    W = 2 * C
    row = lax.broadcasted_iota(jnp.int32, (C, W), 0)
    lane = lax.broadcasted_iota(jnp.int32, (C, W), 1)
    left = lane < C
    col = lane % C
    tri_incl = row >= col
    tri_strict = row > col
    eye = (row == col).astype(F32)
    left_row = left[0:1, :]
    grows = group * C
    rr = lax.broadcasted_iota(jnp.int32, (grows, grows), 0)
    cc = lax.broadcasted_iota(jnp.int32, (grows, grows), 1)
    cum_down = (((rr // C) == (cc // C)) & (rr >= cc)).astype(F32)
    rr2 = lax.broadcasted_iota(jnp.int32, (grows, 2 * grows), 0)
    cc2 = lax.broadcasted_iota(jnp.int32, (grows, 2 * grows), 1)
    cum_right = (((rr2 // C) == (cc2 // W)) & (rr2 <= (cc2 // W) * C + cc2 % C)).astype(F32)
    ng = ng_ref[...]
    zeros_c = jnp.zeros((C, GDN_DK), F32)
    zeros_s = jnp.zeros((GDN_DK, GDN_DV), F32)
    pairs = range(GDN_HEADS // 2)
    batches = range(nbatch)

    def wide(xa, xb):
        return jnp.concatenate([jnp.broadcast_to(xa, (C, GDN_DK)), jnp.broadcast_to(xb, (C, GDN_DK))], 1)

    def block_diag(x):
        return jnp.concatenate([jnp.where(left, x, 0.0), jnp.where(left, 0.0, x)], 0)

    def prepare(chunks, out):
        g0 = chunks[0] * C
        gcum_c, gcum_r, gb_grp = [], [], []
        for b in batches:
            gb_grp.append(gbcol_ref[b, g0:g0 + grows, :])
            gcum_c.append(_dot(cum_down, gb_grp[b], HIGHEST))
            gcum_r.append(_dot(gbrow_ref[b, :, g0:g0 + grows], cum_right, HIGHEST))
        low, attn, rhs, qdec, kdec, egl = [], [], [], [], [], []
        for ci, b, pr in [(ci, b, pr) for ci in chunks for b in batches for pr in pairs]:
            r0, n0, lo = ci * C, ci * C - g0, pr * 2 * GDN_DK
            ha, hb = 2 * pr, 2 * pr + 1
            gc_a = gcum_c[b][n0:n0 + C, G_LANE + ha:G_LANE + ha + 1]
            gc_b = gcum_c[b][n0:n0 + C, G_LANE + hb:G_LANE + hb + 1]
            gr_w = gcum_r[b][:, 2 * n0:2 * n0 + W]
            gr = jnp.where(left_row, gr_w[GDN_HEADS + ha:GDN_HEADS + ha + 1, :],
                           gr_w[GDN_HEADS + hb:GDN_HEADS + hb + 1, :])
            gl_a, gl_b = gc_a[C - 1:C, :], gc_b[C - 1:C, :]
            beta = wide(gb_grp[b][n0:n0 + C, BETA_LANE + ha:BETA_LANE + ha + 1],
                        gb_grp[b][n0:n0 + C, BETA_LANE + hb:BETA_LANE + hb + 1])
            decay = jnp.exp(jnp.where(tri_incl, jnp.where(left, gc_a, gc_b) - gr, -jnp.inf))
            eg = wide(jnp.exp(gc_a), jnp.exp(gc_b))
            qp = q_ref[b, r0:r0 + C, lo:lo + 2 * GDN_DK].astype(F32)
            kp = k_ref[b, r0:r0 + C, lo:lo + 2 * GDN_DK].astype(F32)
            vp = v_ref[b, r0:r0 + C, lo:lo + 2 * GDN_DV].astype(F32)
            kb = kp * beta
            k_bd = jnp.concatenate([jnp.concatenate([kp[:, :GDN_DK], zeros_c], 1),
                                    jnp.concatenate([zeros_c, kp[:, GDN_DK:]], 1)], 0)
            kq = _bdot_nt(jnp.concatenate([kb, qp], 0), k_bd)
            low.append(jnp.where(tri_strict, kq[:C] * decay, 0.0))
            attn.append(jnp.where(tri_incl, kq[C:] * decay, 0.0))
            vb, kbeg = vp * beta, kb * eg
            rhs.append(jnp.concatenate(
                [jnp.concatenate([vb[:, :GDN_DV], kbeg[:, :GDN_DK], zeros_c, zeros_c], 1),
                 jnp.concatenate([zeros_c, zeros_c, vb[:, GDN_DV:], kbeg[:, GDN_DK:]], 1)], 0))
            qdec.append(qp * eg)
            kdec.append(kp * wide(jnp.exp(gl_a - gc_a), jnp.exp(gl_b - gc_b)))
            egl.append((jnp.exp(gl_a), jnp.exp(gl_b)))
        yield
        items = range(len(low))
        inv = [eye - low[it] for it in items]
        pw = [_bdot(low[it], block_diag(low[it])) for it in items]
        yield
        span = 2
        while span < C:
            if 2 * span >= C:
                inv = [inv[it] + _bdot(inv[it], block_diag(pw[it])) for it in items]
            else:
                both = [_bdot(jnp.concatenate([inv[it], pw[it]], 0), block_diag(pw[it]))
                        for it in items]
                inv = [inv[it] + both[it][:C] for it in items]
                pw = [both[it][C:] for it in items]
            span *= 2
            yield
        sol = [_bdot(inv[it], rhs[it]) for it in items]
        per_chunk = nbatch * len(pairs)
        for n, ci in enumerate(chunks):
            sl = slice(n * per_chunk, (n + 1) * per_chunk)
            out[ci] = dict(
                u=[jnp.concatenate([x[:, :GDN_DV], x[:, 2 * GDN_DV:3 * GDN_DV]], 1) for x in sol[sl]],
                w=[jnp.concatenate([x[:, GDN_DV:2 * GDN_DV], x[:, 3 * GDN_DV:]], 1) for x in sol[sl]],
                attn=attn[sl], qdec=qdec[sl], kdec=kdec[sl], egl=egl[sl])
        yield

    def recur_chunk(ci, pre, s):
        r0 = ci * C
        items = [(b, pr) for b in batches for pr in pairs]
        s_bd = [jnp.concatenate([jnp.concatenate([s[b][2 * pr], zeros_s], 1),
                                 jnp.concatenate([zeros_s, s[b][2 * pr + 1]], 1)], 0)
                for b, pr in items]
        ws = [_bdot(jnp.concatenate([pre["w"][n], pre["qdec"][n]], 0), s_bd[n])
              for n in range(len(items))]
        yield
        v_new = [pre["u"][n] - ws[n][:C] for n in range(len(items))]
        v_bd = [jnp.concatenate([jnp.concatenate([v[:, :GDN_DV], zeros_c], 1),
                                 jnp.concatenate([zeros_c, v[:, GDN_DV:]], 1)], 0) for v in v_new]
        o = [ws[n][C:] + _bdot(pre["attn"][n], v_bd[n]) for n in range(len(items))]
        for n, (b, pr) in enumerate(items):
            for half in range(2):
                lo = half * GDN_DK
                s[b][2 * pr + half] = s[b][2 * pr + half] * pre["egl"][n][half] + _bdot_tn(
                    pre["kdec"][n][:, lo:lo + GDN_DK], v_new[n][:, lo:lo + GDN_DV])
        yield
        for n, (b, pr) in enumerate(items):
            for half in range(2):
                hd = 2 * pr + half
                oh = o[n][:, half * GDN_DV:(half + 1) * GDN_DV]
                zh = z_ref[b, r0:r0 + C, hd * GDN_DV:(hd + 1) * GDN_DV].astype(F32)
                on = oh * lax.rsqrt(jnp.mean(oh * oh, -1, keepdims=True) + RMS_EPS) * ng
                o_ref[b, r0:r0 + C, hd * GDN_DV:(hd + 1) * GDN_DV] = (on * _silu(zh)).astype(BF16)
        yield

    def recur(chunks, prepared, s):
        for ci in chunks:
            yield from recur_chunk(ci, prepared[ci], s)

    s = [[state[b, hd] for hd in range(GDN_HEADS)] for b in batches]
    groups = [list(range(g, min(g + group, nchunk))) for g in range(0, nchunk, group)]
    prepared = {}
    for _ in prepare(groups[0], prepared):
        pass
    for gi, chunks in enumerate(groups):
        nxt = prepare(groups[gi + 1], prepared) if gi + 1 < len(groups) else iter(())
        cur = recur(chunks, prepared, s)
        nxt_done = cur_done = False
        while not (nxt_done and cur_done):
            nxt_done = nxt_done or next(nxt, "end") == "end"
            cur_done = cur_done or next(cur, "end") == "end"
    for b in batches:
        for hd in range(GDN_HEADS):
            state[b, hd] = s[b][hd]


def _gdn_call(qg, kg, vg, zg, gbcol, gbrow, layer, norm_g, *, rows, group):
    B, S, _ = qg.shape
    row = lambda w: pl.BlockSpec((B, rows, w), lambda i: (0, i, 0))
    kern = functools.partial(_gdn_kernel, nchunk=rows // CHUNK, group=group)
    return pl.pallas_call(
        kern, out_shape=jax.ShapeDtypeStruct((B, S, GDN_VW), BF16), grid=(S // rows,),
        in_specs=[row(GDN_QK), row(GDN_QK), row(GDN_VW), row(GDN_VW), row(LANES),
                  pl.BlockSpec((B, 2 * GDN_HEADS, rows), lambda i: (0, 0, i)),
                  _layer_spec(norm_g, layer)],
        out_specs=row(GDN_VW),
        scratch_shapes=[pltpu.VMEM((B, GDN_HEADS, GDN_DK, GDN_DV), F32)],
        compiler_params=pltpu.CompilerParams(dimension_semantics=("arbitrary",),
                                             vmem_limit_bytes=V7X_VMEM_LIMIT),
        name="gdn",
    )(qg, kg, vg, zg, gbcol, gbrow, norm_g)


def _attn_kernel(qt_ref, k_ref, vt_ref, o_ref, m_ref, acc_ref, st_ref, *, blk, hps):
    i = pl.program_id(2)
    m_ref[...] = jnp.full_like(m_ref, -jnp.inf)
    acc_ref[...] = jnp.zeros_like(acc_ref)

    def scores(hd, j, masked=False):
        start = pl.multiple_of(j * blk, blk)
        s = _dot(k_ref[0, hd, pl.ds(start, blk), :], qt_ref[0, hd, 0]).astype(BF16)
        if masked:
            kc = lax.broadcasted_iota(jnp.int32, (blk, blk), 0) // CHUNK
            qc = lax.broadcasted_iota(jnp.int32, (blk, blk), 1) // CHUNK
            s = jnp.where(kc <= qc, s, -jnp.inf)
        st_ref[hd] = s

    def process(j, nxt):
        for hd in range(hps):
            s = st_ref[hd]
            if nxt is not None:
                scores(hd, nxt)
            m_old = m_ref[hd]
            m_new = jnp.maximum(m_old, jnp.max(s, axis=0, keepdims=True).astype(F32))
            alpha = jnp.exp2(m_old - m_new)
            p = jnp.exp2(s - m_new.astype(BF16))
            acc_ref[hd] = alpha * acc_ref[hd] + _dot(vt_ref[0, hd, j], p)
            m_ref[hd] = m_new

    for hd in range(hps):
        scores(hd, i, masked=True)
    process(i, 0)

    @pl.when(i > 0)
    def _():
        lax.fori_loop(0, i - 1, lambda j, carry: (process(j, j + 1), carry)[1], 0)
        process(i - 1, None)

    for hd in range(hps):
        out = acc_ref[hd, 0:MLA_V, :] / acc_ref[hd, MLA_V:MLA_V + 1, :]
        o_ref[0, :, hd * MLA_V:(hd + 1) * MLA_V] = out.T.astype(BF16)


def _attn_call(qt, ka, vt, *, blk, hps):
    B, H, S, _ = ka.shape
    nb = S // blk
    kern = functools.partial(_attn_kernel, blk=blk, hps=hps)
    return pl.pallas_call(
        kern, out_shape=jax.ShapeDtypeStruct((B, S, H * MLA_V), BF16), grid=(B, H // hps, nb),
        in_specs=[pl.BlockSpec((1, hps, 1, ATT_DK, blk), lambda b, h, i: (b, h, i, 0, 0)),
                  pl.BlockSpec((1, hps, S, ATT_DK), lambda b, h, i: (b, h, 0, 0),
                               pipeline_mode=pl.Buffered(1)),
                  pl.BlockSpec((1, hps, nb, VT_ROWS, blk), lambda b, h, i: (b, h, 0, 0, 0),
                               pipeline_mode=pl.Buffered(1))],
        out_specs=pl.BlockSpec((1, blk, hps * MLA_V), lambda b, h, i: (b, i, h)),
        scratch_shapes=[pltpu.VMEM((hps, 1, blk), F32), pltpu.VMEM((hps, VT_ROWS, blk), F32),
                        pltpu.VMEM((hps, blk, blk), BF16)],
        compiler_params=pltpu.CompilerParams(
            dimension_semantics=("arbitrary", "arbitrary", "arbitrary"),
            vmem_limit_bytes=V7X_VMEM_LIMIT),
        name="attn",
    )(qt, ka, vt)


def _layer_norm(x, g, b):
    mu = jnp.mean(x, -1, keepdims=True)
    xc = x - mu
    var = jnp.mean(xc * xc, -1, keepdims=True)
    return xc * lax.rsqrt(var + LN_EPS) * g + b


def _post_kernel(x_ref, og_ref, om_ref, p_ref, wog_ref, wom_ref, l1g_ref, l1b_ref, wg_ref, wu_ref,
                 wd_ref, l2g_ref, l2b_ref, wple_ref, wpg_ref, o_ref, *, alpha, ff_chunk):
    mix = _dot(og_ref[...], wog_ref[...]) + _dot(om_ref[...], wom_ref[...])
    x1 = _layer_norm(alpha * x_ref[...] + mix, l1g_ref[...], l1b_ref[...])
    x1b = x1.astype(BF16)
    d_ff = wd_ref.shape[0]
    ff = jnp.zeros_like(x1)
    for c0 in range(0, d_ff, ff_chunk):
        c1 = min(c0 + ff_chunk, d_ff)
        gate = _dot(x1b, wg_ref[:, c0:c1])
        up = _dot(x1b, wu_ref[:, c0:c1])
        ff = ff + _dot((_silu(gate) * up).astype(BF16), wd_ref[c0:c1, :])
    x2 = _layer_norm(alpha * x1 + ff, l2g_ref[...], l2b_ref[...])
    emb = _dot(p_ref[...].astype(BF16), wple_ref[...])
    o_ref[...] = x2 + _sigmoid(_dot(x2.astype(BF16), wpg_ref[...])) * emb


def _post_call(x2d, og, om, p3d, layer, w_out, ln1_g, ln1_b, w_gate_up, w_down, ln2_g, ln2_b, w_ple,
               w_pg, *, tm, alpha, ff_chunk):
    T, D = x2d.shape
    row = lambda w: pl.BlockSpec((tm, w), lambda i: (i, 0))
    weights = ((w_out, (0, 2, 0)), (w_out, (0, 2, 1)), (ln1_g, None), (ln1_b, None),
               (w_gate_up, (1, 2, 0)), (w_gate_up, (1, 2, 1)), (w_down, None), (ln2_g, None),
               (ln2_b, None), (w_ple, None), (w_pg, None))
    kern = functools.partial(_post_kernel, alpha=alpha, ff_chunk=ff_chunk)
    return pl.pallas_call(
        kern, out_shape=jax.ShapeDtypeStruct((T, D), F32), grid=(T // tm,),
        in_specs=[row(D), row(og.shape[1]), row(om.shape[1]),
                  pl.BlockSpec((None, tm, p3d.shape[2]), lambda i: (layer, i, 0))]
        + [_layer_spec(w, layer, split) for w, split in weights],
        out_specs=row(D),
        compiler_params=pltpu.CompilerParams(dimension_semantics=("arbitrary",),
                                             vmem_limit_bytes=V7X_VMEM_LIMIT),
        name="post",
    )(x2d, og, om, p3d, *[w for w, _ in weights])


def _reorder_weights(w_in, w_uq, w_ukv, a_log, dt_bias):
    depth, d, _ = w_in.shape
    q, k, v, z, b, a, cq, ckv, kr = jnp.split(
        w_in.astype(BF16),
        np.cumsum((GDN_QK, GDN_QK, GDN_VW, GDN_VW, GDN_HEADS, GDN_HEADS, Q_LORA, KV_LORA)).tolist(),
        axis=-1)
    pad = jnp.zeros((depth, d, LANES - MLA_ROPE - 2 * GDN_HEADS), BF16)
    w_in_r = jnp.concatenate([q, k, v, z, cq, ckv, kr, b, a, pad], -1)

    uq = w_uq.reshape(depth, Q_LORA, MLA_HEADS, MLA_NOPE + MLA_ROPE)
    uq_nope = uq[..., :MLA_NOPE].reshape(depth, Q_LORA, MLA_HEADS * MLA_NOPE)
    lane_pad = ((0, 0), (0, 0), (0, 0), (0, LANES - MLA_ROPE))
    rope_cols = uq[..., MLA_NOPE:]
    partner_cols = jnp.concatenate([rope_cols[..., MLA_ROPE // 2:], rope_cols[..., :MLA_ROPE // 2]], -1)
    w_uq_r = jnp.concatenate(
        [uq_nope, jnp.pad(rope_cols, lane_pad).reshape(depth, Q_LORA, MLA_HEADS * LANES),
         jnp.pad(partner_cols, lane_pad).reshape(depth, Q_LORA, MLA_HEADS * LANES)], -1).astype(BF16)

    ukv = w_ukv.reshape(depth, KV_LORA, MLA_HEADS, MLA_NOPE + MLA_V)
    w_ukv_r = jnp.concatenate([ukv[..., :MLA_NOPE].reshape(depth, KV_LORA, -1),
                               ukv[..., MLA_NOPE:].reshape(depth, KV_LORA, -1)], -1).astype(BF16)

    lanes_pad = ((0, 0), (0, 0), (G_LANE, LANES - G_LANE - GDN_HEADS))
    alog_pad = jnp.pad(a_log[:, None, :], lanes_pad)
    dtb_pad = jnp.pad(dt_bias[:, None, :], lanes_pad)
    return w_in_r, w_uq_r, w_ukv_r, alog_pad, dtb_pad


def _rope_tables(positions):
    inv_freq = ROPE_THETA ** (-jnp.arange(0, MLA_ROPE, 2, dtype=F32) / MLA_ROPE)
    ang = positions.astype(F32)[..., None] * inv_freq
    cos, sin = jnp.cos(ang), jnp.sin(ang)
    reps = LANES // MLA_ROPE
    return (jnp.tile(jnp.concatenate([cos, cos], -1), (1, 1, reps)),
            jnp.tile(jnp.concatenate([-sin, sin], -1), (1, 1, reps)))


def _tile_rows(s, want):
    t = min(want, s)
    while s % t:
        t //= 2
    return t


def kernel(x, p, positions, w_in, conv_w, a_log, dt_bias, gdn_norm_g, q_norm_g, w_uq, kv_norm_g,
           w_ukv, w_out, ln1_g, ln1_b, w_gate_up, w_down, ln2_g, ln2_b, w_ple, w_ple_gate):
    B, S, D = x.shape
    depth = w_in.shape[0]
    d_ff = w_down.shape[1]
    alpha = (2.0 * depth) ** 0.25
    tm = _tile_rows(S, 512)
    gdn_rows = _tile_rows(S, 256)
    ff_chunk = -(-d_ff // (2 * V7X_MXU_COLS)) * V7X_MXU_COLS

    w_in_r, w_uq_r, w_ukv_r, alog_pad, dtb_pad = _reorder_weights(w_in, w_uq, w_ukv, a_log, dt_bias)
    cos_t, sin_t = _rope_tables(positions)
    w_out_b = w_out.astype(BF16)
    w_gu_b = w_gate_up.astype(BF16)
    w_down_b = w_down.astype(BF16)
    w_ple_b = w_ple.astype(BF16)
    w_pg_b = w_ple_gate.astype(BF16)
    vec = lambda a: a[:, None, :]
    p3d = p.reshape(depth, B * S, -1)

    for i in range(depth):
        qg, kg, vg, zg, gbcol, gbrow, qt, ka, vt = _prep_call(
            x, i, w_in_r, conv_w, alog_pad, dtb_pad, vec(q_norm_g), vec(kv_norm_g), w_uq_r, w_ukv_r,
            cos_t, sin_t, tm=tm)
        og = _gdn_call(qg, kg, vg, zg, gbcol, gbrow, i, vec(gdn_norm_g), rows=gdn_rows, group=2)
        om = _attn_call(qt, ka, vt, blk=tm, hps=MLA_HEADS)
        x = _post_call(
            x.reshape(B * S, D), og.reshape(B * S, -1), om.reshape(B * S, -1), p3d, i, w_out_b,
            vec(ln1_g), vec(ln1_b), w_gu_b, w_down_b, vec(ln2_g), vec(ln2_b), w_ple_b, w_pg_b,
            tm=tm, alpha=alpha, ff_chunk=ff_chunk,
        ).reshape(B, S, D)
    return x
```

```python
import functools

import jax
import jax.numpy as jnp
import numpy as np
from jax import lax
from jax.experimental import pallas as pl
from jax.experimental.pallas import tpu as pltpu

F32 = jnp.float32
BF16 = jnp.bfloat16

CHUNK = 64
GDN_HEADS = 4
GDN_DK = 128
GDN_DV = 128
CONV_WIDTH = 4
MLA_HEADS = 4
MLA_NOPE = 128
MLA_ROPE = 64
MLA_V = 128
Q_LORA = 384
KV_LORA = 256
ROPE_THETA = 10000.0
LN_EPS = 1e-5
RMS_EPS = 1e-6

GDN_QK = GDN_HEADS * GDN_DK
GDN_VW = GDN_HEADS * GDN_DV
QKV_W = 2 * GDN_QK + GDN_VW
LANES = 128
Z_OFF = QKV_W
CQ_OFF = Z_OFF + GDN_VW
CKV_OFF = CQ_OFF + Q_LORA
LAST_OFF = CKV_OFF + KV_LORA
IN_PAD = LAST_OFF + LANES
BETA_LANE = MLA_ROPE
G_LANE = MLA_ROPE + GDN_HEADS
ATT_DK = 2 * LANES
F32_SUBLANES = 8
BF16_SUBLANES = 16
VT_ROWS = MLA_V + BF16_SUBLANES
CONV_COLS = GDN_QK

V7X_VMEM_LIMIT = 56 * 1024 * 1024
V7X_MXU_COLS = 256

HIGHEST = lax.Precision.HIGHEST


def _dot(a, b, precision=None):
    return jnp.dot(a, b, preferred_element_type=F32, precision=precision)


def _dot_nt(a, b, precision=None):
    return lax.dot_general(a, b, (((1,), (1,)), ((), ())), preferred_element_type=F32,
                           precision=precision)


def _dot_tn(a, b, precision=None):
    return lax.dot_general(a, b, (((0,), (0,)), ((), ())), preferred_element_type=F32,
                           precision=precision)


def _bdot(a, b):
    return _dot(a.astype(BF16), b.astype(BF16))


def _bdot_nt(a, b):
    return _dot_nt(a.astype(BF16), b.astype(BF16))


def _bdot_tn(a, b):
    return _dot_tn(a.astype(BF16), b.astype(BF16))


def _sigmoid(x):
    return 1.0 / (1.0 + jnp.exp(-x))


def _silu(x):
    return x * _sigmoid(x)


def _softplus(x):
    return jnp.maximum(x, 0.0) + jnp.log(1.0 + jnp.exp(-jnp.abs(x)))


def _layer_spec(arr, layer, split=None):
    block = list(arr.shape[1:])
    index = [0] * len(block)
    if split is not None:
        axis, parts, which = split
        block[axis] //= parts
        index[axis] = which
    return pl.BlockSpec((None, *block), lambda *_: (layer, *index), pipeline_mode=pl.Buffered(1))


def _prep_kernel(x_ref, w_in_ref, conv_w_ref, alog_ref, dtb_ref, qng_ref, kvng_ref, w_uq_ref,
                 w_ukv_ref, cos_ref, sin_ref,
                 qg_ref, kg_ref, vg_ref, zg_ref, gbcol_ref, gbrow_ref, qt_ref, ka_ref, vt_ref,
                 cbuf, *, tm, att_scale):
    @pl.when(pl.program_id(1) == 0)
    def _():
        cbuf[...] = jnp.zeros_like(cbuf)

    top_rows = lax.broadcasted_iota(jnp.int32, (F32_SUBLANES, CONV_COLS), 0)

    xb = x_ref[0].astype(BF16)
    lane = lax.broadcasted_iota(jnp.int32, (tm, LANES), 1)
    cos = cos_ref[0]
    sin = sin_ref[0]
    first_half = (lane % MLA_ROPE) < (MLA_ROPE // 2)
    nh = MLA_HEADS * MLA_NOPE

    def proj(c0, c1):
        return _dot(xb, w_in_ref[:, c0:c1])

    def rope(blk):
        partner = jnp.where(first_half, pltpu.roll(blk, LANES - MLA_ROPE // 2, 1),
                            pltpu.roll(blk, MLA_ROPE // 2, 1))
        return blk * cos + partner * sin

    def conv_group(grp, part, out_ref):
        c0 = grp * GDN_QK + part * CONV_COLS
        c1 = c0 + CONV_COLS
        o0 = part * CONV_COLS
        pre = proj(c0, c1)
        cw = conv_w_ref[:, c0:c1]
        y = pre * cw[0:1, :]
        for j in range(1, CONV_WIDTH):
            carry_in = cbuf[j - 1:j, c0:c1]
            cbuf[j - 1:j, c0:c1] = y[tm - 1:tm, :]
            shifted = pltpu.roll(y, 1, 0)
            top = jnp.where(top_rows == 0, carry_in, shifted[0:F32_SUBLANES, :])
            y = jnp.concatenate([top, shifted[F32_SUBLANES:, :]], 0) + pre * cw[j:j + 1, :]
        y = _silu(y)
        if out_ref is vg_ref:
            out_ref[0, :, o0:o0 + CONV_COLS] = y.astype(BF16)
            return
        out_scale = GDN_DK ** -0.5 if out_ref is qg_ref else 1.0
        for lo in range(0, CONV_COLS, GDN_DK):
            yh = y[:, lo:lo + GDN_DK]
            yn = yh * (lax.rsqrt(jnp.sum(yh * yh, -1, keepdims=True) + RMS_EPS) * out_scale)
            out_ref[0, :, o0 + lo:o0 + lo + GDN_DK] = yn.astype(BF16)

    for grp, out_ref in enumerate((qg_ref, kg_ref, vg_ref)):
        for part in range(GDN_QK // CONV_COLS):
            conv_group(grp, part, out_ref)

    latent = proj(CQ_OFF, IN_PAD)
    cq = latent[:, :Q_LORA]
    cqn = cq * lax.rsqrt(jnp.mean(cq * cq, -1, keepdims=True) + RMS_EPS) * qng_ref[...]
    qm = _dot(cqn.astype(BF16), w_uq_ref[...])
    for hd in range(MLA_HEADS):
        lo = hd * LANES
        q_rope = qm[:, nh + lo:nh + lo + LANES] * cos + qm[:, 2 * nh + lo:2 * nh + lo + LANES] * sin
        q_full = jnp.concatenate([qm[:, lo:lo + LANES], q_rope], -1) * att_scale
        qt_ref[0, hd, 0] = q_full.T.astype(BF16)

    ckv = latent[:, Q_LORA:Q_LORA + KV_LORA]
    last = latent[:, Q_LORA + KV_LORA:]
    ckvn = ckv * lax.rsqrt(jnp.mean(ckv * ckv, -1, keepdims=True) + RMS_EPS) * kvng_ref[...]
    kvm = _dot(ckvn.astype(BF16), w_ukv_ref[...])
    k_rope = jnp.where(lane < MLA_ROPE, rope(last), 0.0)
    ones_rows = (lax.broadcasted_iota(jnp.int32, (VT_ROWS - MLA_V, tm), 0) == 0).astype(BF16)
    for hd in range(MLA_HEADS):
        lo = hd * LANES
        ka_ref[0, hd] = jnp.concatenate([kvm[:, lo:lo + LANES], k_rope], -1).astype(BF16)
        vt_ref[0, hd, 0, 0:MLA_V, :] = kvm[:, nh + lo:nh + lo + LANES].T.astype(BF16)
        vt_ref[0, hd, 0, MLA_V:VT_ROWS, :] = ones_rows

    beta = _sigmoid(last)
    g = -jnp.exp(alog_ref[...]) * _softplus(last + dtb_ref[...])
    gb = jnp.where((lane >= BETA_LANE) & (lane < G_LANE), beta,
                   jnp.where((lane >= G_LANE) & (lane < G_LANE + GDN_HEADS), g, 0.0))
    gbcol_ref[0] = gb
    gbrow_ref[0] = gb.T[BETA_LANE:BETA_LANE + 2 * GDN_HEADS, :]

    zg_ref[0] = proj(Z_OFF, Z_OFF + GDN_VW).astype(BF16)


def _prep_call(x, layer, w_in_r, conv_w, alog_pad, dtb_pad, qn_g, kvn_g, w_uq_r, w_ukv_r, cos_t,
               sin_t, *, tm):
    B, S, D = x.shape
    params = (w_in_r, conv_w, alog_pad, dtb_pad, qn_g, kvn_g, w_uq_r, w_ukv_r)
    nt = S // tm
    row = lambda w: pl.BlockSpec((1, tm, w), lambda b, i: (b, i, 0))
    head = pl.BlockSpec((1, MLA_HEADS, tm, ATT_DK), lambda b, i: (b, 0, i, 0))
    out_shape = (
        jax.ShapeDtypeStruct((B, S, GDN_QK), BF16), jax.ShapeDtypeStruct((B, S, GDN_QK), BF16),
        jax.ShapeDtypeStruct((B, S, GDN_VW), BF16), jax.ShapeDtypeStruct((B, S, GDN_VW), BF16),
        jax.ShapeDtypeStruct((B, S, LANES), F32), jax.ShapeDtypeStruct((B, 2 * GDN_HEADS, S), F32),
        jax.ShapeDtypeStruct((B, MLA_HEADS, nt, ATT_DK, tm), BF16),
        jax.ShapeDtypeStruct((B, MLA_HEADS, S, ATT_DK), BF16),
        jax.ShapeDtypeStruct((B, MLA_HEADS, nt, VT_ROWS, tm), BF16),
    )
    transposed = lambda r: pl.BlockSpec((1, MLA_HEADS, 1, r, tm), lambda b, i: (b, 0, i, 0, 0))
    out_specs = (
        row(GDN_QK), row(GDN_QK), row(GDN_VW), row(GDN_VW), row(LANES),
        pl.BlockSpec((1, 2 * GDN_HEADS, tm), lambda b, i: (b, 0, i)),
        transposed(ATT_DK), head, transposed(VT_ROWS),
    )
    in_specs = [row(D)] + [_layer_spec(a, layer) for a in params] + [row(LANES), row(LANES)]
    kern = functools.partial(_prep_kernel, tm=tm,
                             att_scale=(MLA_NOPE + MLA_ROPE) ** -0.5 * float(np.log2(np.e)))
    return pl.pallas_call(
        kern, out_shape=out_shape, grid=(B, nt), in_specs=in_specs, out_specs=out_specs,
        scratch_shapes=[pltpu.VMEM((F32_SUBLANES, QKV_W), F32)],
        compiler_params=pltpu.CompilerParams(dimension_semantics=("arbitrary", "arbitrary"),
                                             vmem_limit_bytes=V7X_VMEM_LIMIT),
        name="prep",
    )(x, *params, cos_t, sin_t)


def _gdn_kernel(q_ref, k_ref, v_ref, z_ref, gbcol_ref, gbrow_ref, ng_ref, o_ref, state, *, nchunk,
                group):
    @pl.when(pl.program_id(0) == 0)
    def _():
        state[...] = jnp.zeros_like(state)

    nbatch = q_ref.shape[0]
    C = CHUNK
    W = 2 * C
    row = lax.broadcasted_iota(jnp.int32, (C, W), 0)
    lane = lax.broadcasted_iota(jnp.int32, (C, W), 1)
    left = lane < C
    col = lane % C
    tri_incl = row >= col
    tri_strict = row > col
    eye = (row == col).astype(F32)
    left_row = left[0:1, :]
    grows = group * C
    rr = lax.broadcasted_iota(jnp.int32, (grows, grows), 0)
    cc = lax.broadcasted_iota(jnp.int32, (grows, grows), 1)
    cum_down = (((rr // C) == (cc // C)) & (rr >= cc)).astype(F32)
    rr2 = lax.broadcasted_iota(jnp.int32, (grows, 2 * grows), 0)
    cc2 = lax.broadcasted_iota(jnp.int32, (grows, 2 * grows), 1)
    cum_right = (((rr2 // C) == (cc2 // W)) & (rr2 <= (cc2 // W) * C + cc2 % C)).astype(F32)
    ng = ng_ref[...]
    zeros_c = jnp.zeros((C, GDN_DK), F32)
    zeros_s = jnp.zeros((GDN_DK, GDN_DV), F32)
    pairs = range(GDN_HEADS // 2)
    batches = range(nbatch)

    def wide(xa, xb):
        return jnp.concatenate([jnp.broadcast_to(xa, (C, GDN_DK)), jnp.broadcast_to(xb, (C, GDN_DK))], 1)

    def block_diag(x):
        return jnp.concatenate([jnp.where(left, x, 0.0), jnp.where(left, 0.0, x)], 0)

    def prepare(chunks, out):
        g0 = chunks[0] * C
        gcum_c, gcum_r, gb_grp = [], [], []
        for b in batches:
            gb_grp.append(gbcol_ref[b, g0:g0 + grows, :])
            gcum_c.append(_dot(cum_down, gb_grp[b], HIGHEST))
            gcum_r.append(_dot(gbrow_ref[b, :, g0:g0 + grows], cum_right, HIGHEST))
        low, attn, rhs, qdec, kdec, egl = [], [], [], [], [], []
        for ci, b, pr in [(ci, b, pr) for ci in chunks for b in batches for pr in pairs]:
            r0, n0, lo = ci * C, ci * C - g0, pr * 2 * GDN_DK
            ha, hb = 2 * pr, 2 * pr + 1
            gc_a = gcum_c[b][n0:n0 + C, G_LANE + ha:G_LANE + ha + 1]
            gc_b = gcum_c[b][n0:n0 + C, G_LANE + hb:G_LANE + hb + 1]
            gr_w = gcum_r[b][:, 2 * n0:2 * n0 + W]
            gr = jnp.where(left_row, gr_w[GDN_HEADS + ha:GDN_HEADS + ha + 1, :],
                           gr_w[GDN_HEADS + hb:GDN_HEADS + hb + 1, :])
            gl_a, gl_b = gc_a[C - 1:C, :], gc_b[C - 1:C, :]
            beta = wide(gb_grp[b][n0:n0 + C, BETA_LANE + ha:BETA_LANE + ha + 1],
                        gb_grp[b][n0:n0 + C, BETA_LANE + hb:BETA_LANE + hb + 1])
            decay = jnp.exp(jnp.where(tri_incl, jnp.where(left, gc_a, gc_b) - gr, -jnp.inf))
            eg = wide(jnp.exp(gc_a), jnp.exp(gc_b))
            qp = q_ref[b, r0:r0 + C, lo:lo + 2 * GDN_DK].astype(F32)
            kp = k_ref[b, r0:r0 + C, lo:lo + 2 * GDN_DK].astype(F32)
            vp = v_ref[b, r0:r0 + C, lo:lo + 2 * GDN_DV].astype(F32)
            kb = kp * beta
            k_bd = jnp.concatenate([jnp.concatenate([kp[:, :GDN_DK], zeros_c], 1),
                                    jnp.concatenate([zeros_c, kp[:, GDN_DK:]], 1)], 0)
            kq = _bdot_nt(jnp.concatenate([kb, qp], 0), k_bd)
            low.append(jnp.where(tri_strict, kq[:C] * decay, 0.0))
            attn.append(jnp.where(tri_incl, kq[C:] * decay, 0.0))
            vb, kbeg = vp * beta, kb * eg
            rhs.append(jnp.concatenate(
                [jnp.concatenate([vb[:, :GDN_DV], kbeg[:, :GDN_DK], zeros_c, zeros_c], 1),
                 jnp.concatenate([zeros_c, zeros_c, vb[:, GDN_DV:], kbeg[:, GDN_DK:]], 1)], 0))
            qdec.append(qp * eg)
            kdec.append(kp * wide(jnp.exp(gl_a - gc_a), jnp.exp(gl_b - gc_b)))
            egl.append((jnp.exp(gl_a), jnp.exp(gl_b)))
        yield
        items = range(len(low))
        inv = [eye - low[it] for it in items]
        pw = [_bdot(low[it], block_diag(low[it])) for it in items]
        yield
        span = 2
        while span < C:
            if 2 * span >= C:
                inv = [inv[it] + _bdot(inv[it], block_diag(pw[it])) for it in items]
            else:
                both = [_bdot(jnp.concatenate([inv[it], pw[it]], 0), block_diag(pw[it]))
                        for it in items]
                inv = [inv[it] + both[it][:C] for it in items]
                pw = [both[it][C:] for it in items]
            span *= 2
            yield
        sol = [_bdot(inv[it], rhs[it]) for it in items]
        per_chunk = nbatch * len(pairs)
        for n, ci in enumerate(chunks):
            sl = slice(n * per_chunk, (n + 1) * per_chunk)
            out[ci] = dict(
                u=[jnp.concatenate([x[:, :GDN_DV], x[:, 2 * GDN_DV:3 * GDN_DV]], 1) for x in sol[sl]],
                w=[jnp.concatenate([x[:, GDN_DV:2 * GDN_DV], x[:, 3 * GDN_DV:]], 1) for x in sol[sl]],
                attn=attn[sl], qdec=qdec[sl], kdec=kdec[sl], egl=egl[sl])
        yield

    def recur_chunk(ci, pre, s):
        r0 = ci * C
        items = [(b, pr) for b in batches for pr in pairs]
        s_bd = [jnp.concatenate([jnp.concatenate([s[b][2 * pr], zeros_s], 1),
                                 jnp.concatenate([zeros_s, s[b][2 * pr + 1]], 1)], 0)
                for b, pr in items]
        ws = [_bdot(jnp.concatenate([pre["w"][n], pre["qdec"][n]], 0), s_bd[n])
              for n in range(len(items))]
        yield
        v_new = [pre["u"][n] - ws[n][:C] for n in range(len(items))]
        v_bd = [jnp.concatenate([jnp.concatenate([v[:, :GDN_DV], zeros_c], 1),
                                 jnp.concatenate([zeros_c, v[:, GDN_DV:]], 1)], 0) for v in v_new]
        o = [ws[n][C:] + _bdot(pre["attn"][n], v_bd[n]) for n in range(len(items))]
        for n, (b, pr) in enumerate(items):
            for half in range(2):
                lo = half * GDN_DK
                s[b][2 * pr + half] = s[b][2 * pr + half] * pre["egl"][n][half] + _bdot_tn(
                    pre["kdec"][n][:, lo:lo + GDN_DK], v_new[n][:, lo:lo + GDN_DV])
        yield
        for n, (b, pr) in enumerate(items):
            for half in range(2):
                hd = 2 * pr + half
                oh = o[n][:, half * GDN_DV:(half + 1) * GDN_DV]
                zh = z_ref[b, r0:r0 + C, hd * GDN_DV:(hd + 1) * GDN_DV].astype(F32)
                on = oh * lax.rsqrt(jnp.mean(oh * oh, -1, keepdims=True) + RMS_EPS) * ng
                o_ref[b, r0:r0 + C, hd * GDN_DV:(hd + 1) * GDN_DV] = (on * _silu(zh)).astype(BF16)
        yield

    def recur(chunks, prepared, s):
        for ci in chunks:
            yield from recur_chunk(ci, prepared[ci], s)

    s = [[state[b, hd] for hd in range(GDN_HEADS)] for b in batches]
    groups = [list(range(g, min(g + group, nchunk))) for g in range(0, nchunk, group)]
    prepared = {}
    for _ in prepare(groups[0], prepared):
        pass
    for gi, chunks in enumerate(groups):
        nxt = prepare(groups[gi + 1], prepared) if gi + 1 < len(groups) else iter(())
        cur = recur(chunks, prepared, s)
        nxt_done = cur_done = False
        while not (nxt_done and cur_done):
            nxt_done = nxt_done or next(nxt, "end") == "end"
            cur_done = cur_done or next(cur, "end") == "end"
    for b in batches:
        for hd in range(GDN_HEADS):
            state[b, hd] = s[b][hd]


def _gdn_call(qg, kg, vg, zg, gbcol, gbrow, layer, norm_g, *, rows, group):
    B, S, _ = qg.shape
    row = lambda w: pl.BlockSpec((B, rows, w), lambda i: (0, i, 0))
    kern = functools.partial(_gdn_kernel, nchunk=rows // CHUNK, group=group)
    return pl.pallas_call(
        kern, out_shape=jax.ShapeDtypeStruct((B, S, GDN_VW), BF16), grid=(S // rows,),
        in_specs=[row(GDN_QK), row(GDN_QK), row(GDN_VW), row(GDN_VW), row(LANES),
                  pl.BlockSpec((B, 2 * GDN_HEADS, rows), lambda i: (0, 0, i)),
                  _layer_spec(norm_g, layer)],
        out_specs=row(GDN_VW),
        scratch_shapes=[pltpu.VMEM((B, GDN_HEADS, GDN_DK, GDN_DV), F32)],
        compiler_params=pltpu.CompilerParams(dimension_semantics=("arbitrary",),
                                             vmem_limit_bytes=V7X_VMEM_LIMIT),
        name="gdn",
    )(qg, kg, vg, zg, gbcol, gbrow, norm_g)


def _attn_kernel(qt_ref, k_ref, vt_ref, o_ref, m_ref, acc_ref, st_ref, *, blk, hps):
    i = pl.program_id(2)
    m_ref[...] = jnp.full_like(m_ref, -jnp.inf)
    acc_ref[...] = jnp.zeros_like(acc_ref)

    def scores(hd, j, masked=False):
        start = pl.multiple_of(j * blk, blk)
        s = _dot(k_ref[0, hd, pl.ds(start, blk), :], qt_ref[0, hd, 0]).astype(BF16)
        if masked:
            kc = lax.broadcasted_iota(jnp.int32, (blk, blk), 0) // CHUNK
            qc = lax.broadcasted_iota(jnp.int32, (blk, blk), 1) // CHUNK
            s = jnp.where(kc <= qc, s, -jnp.inf)
        st_ref[hd] = s

    def process(j, nxt):
        for hd in range(hps):
            s = st_ref[hd]
            if nxt is not None:
                scores(hd, nxt)
            m_old = m_ref[hd]
            m_new = jnp.maximum(m_old, jnp.max(s, axis=0, keepdims=True).astype(F32))
            alpha = jnp.exp2(m_old - m_new)
            p = jnp.exp2(s - m_new.astype(BF16))
            acc_ref[hd] = alpha * acc_ref[hd] + _dot(vt_ref[0, hd, j], p)
            m_ref[hd] = m_new

    for hd in range(hps):
        scores(hd, i, masked=True)
    process(i, 0)

    @pl.when(i > 0)
    def _():
        lax.fori_loop(0, i - 1, lambda j, carry: (process(j, j + 1), carry)[1], 0)
        process(i - 1, None)

    for hd in range(hps):
        out = acc_ref[hd, 0:MLA_V, :] / acc_ref[hd, MLA_V:MLA_V + 1, :]
        o_ref[0, :, hd * MLA_V:(hd + 1) * MLA_V] = out.T.astype(BF16)


def _attn_call(qt, ka, vt, *, blk, hps):
    B, H, S, _ = ka.shape
    nb = S // blk
    kern = functools.partial(_attn_kernel, blk=blk, hps=hps)
    return pl.pallas_call(
        kern, out_shape=jax.ShapeDtypeStruct((B, S, H * MLA_V), BF16), grid=(B, H // hps, nb),
        in_specs=[pl.BlockSpec((1, hps, 1, ATT_DK, blk), lambda b, h, i: (b, h, i, 0, 0)),
                  pl.BlockSpec((1, hps, S, ATT_DK), lambda b, h, i: (b, h, 0, 0)),
                  pl.BlockSpec((1, hps, nb, VT_ROWS, blk), lambda b, h, i: (b, h, 0, 0, 0),
                               pipeline_mode=pl.Buffered(1))],
        out_specs=pl.BlockSpec((1, blk, hps * MLA_V), lambda b, h, i: (b, i, h)),
        scratch_shapes=[pltpu.VMEM((hps, 1, blk), F32), pltpu.VMEM((hps, VT_ROWS, blk), F32),
                        pltpu.VMEM((hps, blk, blk), BF16)],
        compiler_params=pltpu.CompilerParams(
            dimension_semantics=("arbitrary", "arbitrary", "arbitrary"),
            vmem_limit_bytes=V7X_VMEM_LIMIT),
        name="attn",
    )(qt, ka, vt)


def _layer_norm(x, g, b):
    mu = jnp.mean(x, -1, keepdims=True)
    xc = x - mu
    var = jnp.mean(xc * xc, -1, keepdims=True)
    return xc * lax.rsqrt(var + LN_EPS) * g + b


def _post_kernel(x_ref, og_ref, om_ref, p_ref, wog_ref, wom_ref, l1g_ref, l1b_ref, wg_ref, wu_ref,
                 wd_ref, l2g_ref, l2b_ref, wple_ref, wpg_ref, o_ref, *, alpha, ff_chunk):
    mix = _dot(og_ref[...], wog_ref[...]) + _dot(om_ref[...], wom_ref[...])
    x1 = _layer_norm(alpha * x_ref[...] + mix, l1g_ref[...], l1b_ref[...])
    x1b = x1.astype(BF16)
    d_ff = wd_ref.shape[0]
    ff = jnp.zeros_like(x1)
    for c0 in range(0, d_ff, ff_chunk):
        c1 = min(c0 + ff_chunk, d_ff)
        gate = _dot(x1b, wg_ref[:, c0:c1])
        up = _dot(x1b, wu_ref[:, c0:c1])
        ff = ff + _dot((_silu(gate) * up).astype(BF16), wd_ref[c0:c1, :])
    x2 = _layer_norm(alpha * x1 + ff, l2g_ref[...], l2b_ref[...])
    emb = _dot(p_ref[...].astype(BF16), wple_ref[...])
    o_ref[...] = x2 + _sigmoid(_dot(x2.astype(BF16), wpg_ref[...])) * emb


def _post_call(x2d, og, om, p3d, layer, w_out, ln1_g, ln1_b, w_gate_up, w_down, ln2_g, ln2_b, w_ple,
               w_pg, *, tm, alpha, ff_chunk):
    T, D = x2d.shape
    row = lambda w: pl.BlockSpec((tm, w), lambda i: (i, 0))
    weights = ((w_out, (0, 2, 0)), (w_out, (0, 2, 1)), (ln1_g, None), (ln1_b, None),
               (w_gate_up, (1, 2, 0)), (w_gate_up, (1, 2, 1)), (w_down, None), (ln2_g, None),
               (ln2_b, None), (w_ple, None), (w_pg, None))
    kern = functools.partial(_post_kernel, alpha=alpha, ff_chunk=ff_chunk)
    return pl.pallas_call(
        kern, out_shape=jax.ShapeDtypeStruct((T, D), F32), grid=(T // tm,),
        in_specs=[row(D), row(og.shape[1]), row(om.shape[1]),
                  pl.BlockSpec((None, tm, p3d.shape[2]), lambda i: (layer, i, 0))]
        + [_layer_spec(w, layer, split) for w, split in weights],
        out_specs=row(D),
        compiler_params=pltpu.CompilerParams(dimension_semantics=("arbitrary",),
                                             vmem_limit_bytes=V7X_VMEM_LIMIT),
        name="post",
    )(x2d, og, om, p3d, *[w for w, _ in weights])


def _reorder_weights(w_in, w_uq, w_ukv, a_log, dt_bias):
    depth, d, _ = w_in.shape
    q, k, v, z, b, a, cq, ckv, kr = jnp.split(
        w_in.astype(BF16),
        np.cumsum((GDN_QK, GDN_QK, GDN_VW, GDN_VW, GDN_HEADS, GDN_HEADS, Q_LORA, KV_LORA)).tolist(),
        axis=-1)
    pad = jnp.zeros((depth, d, LANES - MLA_ROPE - 2 * GDN_HEADS), BF16)
    w_in_r = jnp.concatenate([q, k, v, z, cq, ckv, kr, b, a, pad], -1)

    uq = w_uq.reshape(depth, Q_LORA, MLA_HEADS, MLA_NOPE + MLA_ROPE)
    uq_nope = uq[..., :MLA_NOPE].reshape(depth, Q_LORA, MLA_HEADS * MLA_NOPE)
    lane_pad = ((0, 0), (0, 0), (0, 0), (0, LANES - MLA_ROPE))
    rope_cols = uq[..., MLA_NOPE:]
    partner_cols = jnp.concatenate([rope_cols[..., MLA_ROPE // 2:], rope_cols[..., :MLA_ROPE // 2]], -1)
    w_uq_r = jnp.concatenate(
        [uq_nope, jnp.pad(rope_cols, lane_pad).reshape(depth, Q_LORA, MLA_HEADS * LANES),
         jnp.pad(partner_cols, lane_pad).reshape(depth, Q_LORA, MLA_HEADS * LANES)], -1).astype(BF16)

    ukv = w_ukv.reshape(depth, KV_LORA, MLA_HEADS, MLA_NOPE + MLA_V)
    w_ukv_r = jnp.concatenate([ukv[..., :MLA_NOPE].reshape(depth, KV_LORA, -1),
                               ukv[..., MLA_NOPE:].reshape(depth, KV_LORA, -1)], -1).astype(BF16)

    lanes_pad = ((0, 0), (0, 0), (G_LANE, LANES - G_LANE - GDN_HEADS))
    alog_pad = jnp.pad(a_log[:, None, :], lanes_pad)
    dtb_pad = jnp.pad(dt_bias[:, None, :], lanes_pad)
    return w_in_r, w_uq_r, w_ukv_r, alog_pad, dtb_pad


def _rope_tables(positions):
    inv_freq = ROPE_THETA ** (-jnp.arange(0, MLA_ROPE, 2, dtype=F32) / MLA_ROPE)
    ang = positions.astype(F32)[..., None] * inv_freq
    cos, sin = jnp.cos(ang), jnp.sin(ang)
    reps = LANES // MLA_ROPE
    return (jnp.tile(jnp.concatenate([cos, cos], -1), (1, 1, reps)),
            jnp.tile(jnp.concatenate([-sin, sin], -1), (1, 1, reps)))


def _tile_rows(s, want):
    t = min(want, s)
    while s % t:
        t //= 2
    return t


def _plan(seq, d_ff):
    return dict(
        tm=_tile_rows(seq, 512),
        gdn_rows=_tile_rows(seq, 4 * CHUNK), gdn_group=2,
        ff_chunk=-(-d_ff // (2 * V7X_MXU_COLS)) * V7X_MXU_COLS,
    )


def kernel(x, p, positions, w_in, conv_w, a_log, dt_bias, gdn_norm_g, q_norm_g, w_uq, kv_norm_g,
           w_ukv, w_out, ln1_g, ln1_b, w_gate_up, w_down, ln2_g, ln2_b, w_ple, w_ple_gate):
    B, S, D = x.shape
    depth = w_in.shape[0]
    d_ff = w_down.shape[1]
    alpha = (2.0 * depth) ** 0.25
    plan = _plan(S, d_ff)
    tm = plan["tm"]

    w_in_r, w_uq_r, w_ukv_r, alog_pad, dtb_pad = _reorder_weights(w_in, w_uq, w_ukv, a_log, dt_bias)
    cos_t, sin_t = _rope_tables(positions)
    w_out_b = w_out.astype(BF16)
    w_gu_b = w_gate_up.astype(BF16)
    w_down_b = w_down.astype(BF16)
    w_ple_b = w_ple.astype(BF16)
    w_pg_b = w_ple_gate.astype(BF16)
    vec = lambda a: a[:, None, :]
    p3d = p.reshape(depth, B * S, -1)

    for i in range(depth):
        qg, kg, vg, zg, gbcol, gbrow, qt, ka, vt = _prep_call(
            x, i, w_in_r, conv_w, alog_pad, dtb_pad, vec(q_norm_g), vec(kv_norm_g), w_uq_r, w_ukv_r,
            cos_t, sin_t, tm=tm)
        og = _gdn_call(qg, kg, vg, zg, gbcol, gbrow, i, vec(gdn_norm_g), rows=plan["gdn_rows"],
                       group=plan["gdn_group"])
        om = _attn_call(qt, ka, vt, blk=tm, hps=MLA_HEADS)
        x = _post_call(
            x.reshape(B * S, D), og.reshape(B * S, -1), om.reshape(B * S, -1), p3d, i, w_out_b,
            vec(ln1_g), vec(ln1_b), w_gu_b, w_down_b, vec(ln2_g), vec(ln2_b), w_ple_b, w_pg_b,
            tm=tm, alpha=alpha, ff_chunk=plan["ff_chunk"],
        ).reshape(B, S, D)
    return x
```

```python
import functools

import jax
import jax.numpy as jnp
import numpy as np
from jax import lax
from jax.experimental import pallas as pl
from jax.experimental.pallas import tpu as pltpu

F32 = jnp.float32
BF16 = jnp.bfloat16

CHUNK = 64
GDN_HEADS = 4
GDN_DK = 128
GDN_DV = 128
CONV_WIDTH = 4
MLA_HEADS = 4
MLA_NOPE = 128
MLA_ROPE = 64
MLA_V = 128
Q_LORA = 384
KV_LORA = 256
ROPE_THETA = 10000.0
LN_EPS = 1e-5
RMS_EPS = 1e-6

GDN_QK = GDN_HEADS * GDN_DK
GDN_VW = GDN_HEADS * GDN_DV
QKV_W = 2 * GDN_QK + GDN_VW
LANES = 128
Z_OFF = QKV_W
CQ_OFF = Z_OFF + GDN_VW
CKV_OFF = CQ_OFF + Q_LORA
LAST_OFF = CKV_OFF + KV_LORA
IN_PAD = LAST_OFF + LANES
BETA_LANE = MLA_ROPE
G_LANE = MLA_ROPE + GDN_HEADS
ATT_DK = 2 * LANES
F32_SUBLANES = 8
BF16_SUBLANES = 16
VT_ROWS = MLA_V + BF16_SUBLANES
CONV_COLS = GDN_QK

V7X_VMEM_LIMIT = 56 * 1024 * 1024
V7X_MXU_COLS = 256

HIGHEST = lax.Precision.HIGHEST


def _dot(a, b, precision=None):
    return jnp.dot(a, b, preferred_element_type=F32, precision=precision)


def _dot_nt(a, b, precision=None):
    return lax.dot_general(a, b, (((1,), (1,)), ((), ())), preferred_element_type=F32,
                           precision=precision)


def _dot_tn(a, b, precision=None):
    return lax.dot_general(a, b, (((0,), (0,)), ((), ())), preferred_element_type=F32,
                           precision=precision)


def _bdot(a, b):
    return _dot(a.astype(BF16), b.astype(BF16))


def _bdot_nt(a, b):
    return _dot_nt(a.astype(BF16), b.astype(BF16))


def _bdot_tn(a, b):
    return _dot_tn(a.astype(BF16), b.astype(BF16))


def _sigmoid(x):
    return 1.0 / (1.0 + jnp.exp(-x))


def _silu(x):
    return x * _sigmoid(x)


def _softplus(x):
    return jnp.maximum(x, 0.0) + jnp.log(1.0 + jnp.exp(-jnp.abs(x)))


def _layer_spec(arr, layer, split=None):
    block = list(arr.shape[1:])
    index = [0] * len(block)
    if split is not None:
        axis, parts, which = split
        block[axis] //= parts
        index[axis] = which
    return pl.BlockSpec((None, *block), lambda *_: (layer, *index), pipeline_mode=pl.Buffered(1))


def _prep_kernel(x_ref, w_in_ref, conv_w_ref, alog_ref, dtb_ref, qng_ref, kvng_ref, w_uq_ref,
                 w_ukv_ref, cos_ref, sin_ref,
                 qg_ref, kg_ref, vg_ref, zg_ref, gbcol_ref, gbrow_ref, qt_ref, ka_ref, vt_ref,
                 cbuf, *, tm, att_scale):
    @pl.when(pl.program_id(1) == 0)
    def _():
        cbuf[...] = jnp.zeros_like(cbuf)

    top_rows = lax.broadcasted_iota(jnp.int32, (F32_SUBLANES, CONV_COLS), 0)

    xb = x_ref[0].astype(BF16)
    lane = lax.broadcasted_iota(jnp.int32, (tm, LANES), 1)
    cos = cos_ref[0]
    sin = sin_ref[0]
    first_half = (lane % MLA_ROPE) < (MLA_ROPE // 2)
    nh = MLA_HEADS * MLA_NOPE

    def proj(c0, c1):
        return _dot(xb, w_in_ref[:, c0:c1])

    def rope(blk):
        partner = jnp.where(first_half, pltpu.roll(blk, LANES - MLA_ROPE // 2, 1),
                            pltpu.roll(blk, MLA_ROPE // 2, 1))
        return blk * cos + partner * sin

    def conv_group(grp, part, out_ref):
        c0 = grp * GDN_QK + part * CONV_COLS
        c1 = c0 + CONV_COLS
        o0 = part * CONV_COLS
        pre = proj(c0, c1)
        cw = conv_w_ref[:, c0:c1]
        y = pre * cw[0:1, :]
        for j in range(1, CONV_WIDTH):
            carry_in = cbuf[j - 1:j, c0:c1]
            cbuf[j - 1:j, c0:c1] = y[tm - 1:tm, :]
            shifted = pltpu.roll(y, 1, 0)
            top = jnp.where(top_rows == 0, carry_in, shifted[0:F32_SUBLANES, :])
            y = jnp.concatenate([top, shifted[F32_SUBLANES:, :]], 0) + pre * cw[j:j + 1, :]
        y = _silu(y)
        if out_ref is vg_ref:
            out_ref[0, :, o0:o0 + CONV_COLS] = y.astype(BF16)
            return
        out_scale = GDN_DK ** -0.5 if out_ref is qg_ref else 1.0
        for lo in range(0, CONV_COLS, GDN_DK):
            yh = y[:, lo:lo + GDN_DK]
            yn = yh * (lax.rsqrt(jnp.sum(yh * yh, -1, keepdims=True) + RMS_EPS) * out_scale)
            out_ref[0, :, o0 + lo:o0 + lo + GDN_DK] = yn.astype(BF16)

    for grp, out_ref in enumerate((qg_ref, kg_ref, vg_ref)):
        for part in range(GDN_QK // CONV_COLS):
            conv_group(grp, part, out_ref)

    latent = proj(CQ_OFF, IN_PAD)
    cq = latent[:, :Q_LORA]
    cqn = cq * lax.rsqrt(jnp.mean(cq * cq, -1, keepdims=True) + RMS_EPS) * qng_ref[...]
    qm = _dot(cqn.astype(BF16), w_uq_ref[...])
    for hd in range(MLA_HEADS):
        lo = hd * LANES
        q_rope = qm[:, nh + lo:nh + lo + LANES] * cos + qm[:, 2 * nh + lo:2 * nh + lo + LANES] * sin
        q_full = jnp.concatenate([qm[:, lo:lo + LANES], q_rope], -1) * att_scale
        qt_ref[0, hd, 0] = q_full.T.astype(BF16)

    ckv = latent[:, Q_LORA:Q_LORA + KV_LORA]
    last = latent[:, Q_LORA + KV_LORA:]
    ckvn = ckv * lax.rsqrt(jnp.mean(ckv * ckv, -1, keepdims=True) + RMS_EPS) * kvng_ref[...]
    kvm = _dot(ckvn.astype(BF16), w_ukv_ref[...])
    k_rope = jnp.where(lane < MLA_ROPE, rope(last), 0.0)
    ones_rows = (lax.broadcasted_iota(jnp.int32, (VT_ROWS - MLA_V, tm), 0) == 0).astype(BF16)
    for hd in range(MLA_HEADS):
        lo = hd * LANES
        ka_ref[0, hd] = jnp.concatenate([kvm[:, lo:lo + LANES], k_rope], -1).astype(BF16)
        vt_ref[0, hd, 0, 0:MLA_V, :] = kvm[:, nh + lo:nh + lo + LANES].T.astype(BF16)
        vt_ref[0, hd, 0, MLA_V:VT_ROWS, :] = ones_rows

    beta = _sigmoid(last)
    g = -jnp.exp(alog_ref[...]) * _softplus(last + dtb_ref[...])
    gb = jnp.where((lane >= BETA_LANE) & (lane < G_LANE), beta,
                   jnp.where((lane >= G_LANE) & (lane < G_LANE + GDN_HEADS), g, 0.0))
    gbcol_ref[0] = gb
    gbrow_ref[0] = gb.T[BETA_LANE:BETA_LANE + 2 * GDN_HEADS, :]

    zg_ref[0] = proj(Z_OFF, Z_OFF + GDN_VW).astype(BF16)


def _prep_call(x, layer, w_in_r, conv_w, alog_pad, dtb_pad, qn_g, kvn_g, w_uq_r, w_ukv_r, cos_t,
               sin_t, *, tm):
    B, S, D = x.shape
    params = (w_in_r, conv_w, alog_pad, dtb_pad, qn_g, kvn_g, w_uq_r, w_ukv_r)
    nt = S // tm
    row = lambda w: pl.BlockSpec((1, tm, w), lambda b, i: (b, i, 0))
    head = pl.BlockSpec((1, MLA_HEADS, tm, ATT_DK), lambda b, i: (b, 0, i, 0))
    out_shape = (
        jax.ShapeDtypeStruct((B, S, GDN_QK), BF16), jax.ShapeDtypeStruct((B, S, GDN_QK), BF16),
        jax.ShapeDtypeStruct((B, S, GDN_VW), BF16), jax.ShapeDtypeStruct((B, S, GDN_VW), BF16),
        jax.ShapeDtypeStruct((B, S, LANES), F32), jax.ShapeDtypeStruct((B, 2 * GDN_HEADS, S), F32),
        jax.ShapeDtypeStruct((B, MLA_HEADS, nt, ATT_DK, tm), BF16),
        jax.ShapeDtypeStruct((B, MLA_HEADS, S, ATT_DK), BF16),
        jax.ShapeDtypeStruct((B, MLA_HEADS, nt, VT_ROWS, tm), BF16),
    )
    transposed = lambda r: pl.BlockSpec((1, MLA_HEADS, 1, r, tm), lambda b, i: (b, 0, i, 0, 0))
    out_specs = (
        row(GDN_QK), row(GDN_QK), row(GDN_VW), row(GDN_VW), row(LANES),
        pl.BlockSpec((1, 2 * GDN_HEADS, tm), lambda b, i: (b, 0, i)),
        transposed(ATT_DK), head, transposed(VT_ROWS),
    )
    in_specs = [row(D)] + [_layer_spec(a, layer) for a in params] + [row(LANES), row(LANES)]
    kern = functools.partial(_prep_kernel, tm=tm,
                             att_scale=(MLA_NOPE + MLA_ROPE) ** -0.5 * float(np.log2(np.e)))
    return pl.pallas_call(
        kern, out_shape=out_shape, grid=(B, nt), in_specs=in_specs, out_specs=out_specs,
        scratch_shapes=[pltpu.VMEM((F32_SUBLANES, QKV_W), F32)],
        compiler_params=pltpu.CompilerParams(dimension_semantics=("arbitrary", "arbitrary"),
                                             vmem_limit_bytes=V7X_VMEM_LIMIT),
        name="prep",
    )(x, *params, cos_t, sin_t)


def _gdn_kernel(q_ref, k_ref, v_ref, z_ref, gbcol_ref, gbrow_ref, ng_ref, o_ref, state, *, nchunk,
                group):
    @pl.when(pl.program_id(0) == 0)
    def _():
        state[...] = jnp.zeros_like(state)

    nbatch = q_ref.shape[0]
    C = CHUNK
    W = 2 * C
    row = lax.broadcasted_iota(jnp.int32, (C, W), 0)
    lane = lax.broadcasted_iota(jnp.int32, (C, W), 1)
    left = lane < C
    col = lane % C
    tri_incl = row >= col
    tri_strict = row > col
    eye = (row == col).astype(F32)
    left_row = left[0:1, :]
    grows = group * C
    rr = lax.broadcasted_iota(jnp.int32, (grows, grows), 0)
    cc = lax.broadcasted_iota(jnp.int32, (grows, grows), 1)
    cum_down = (((rr // C) == (cc // C)) & (rr >= cc)).astype(F32)
    rr2 = lax.broadcasted_iota(jnp.int32, (grows, 2 * grows), 0)
    cc2 = lax.broadcasted_iota(jnp.int32, (grows, 2 * grows), 1)
    cum_right = (((rr2 // C) == (cc2 // W)) & (rr2 <= (cc2 // W) * C + cc2 % C)).astype(F32)
    ng = ng_ref[...]
    zeros_c = jnp.zeros((C, GDN_DK), F32)
    zeros_s = jnp.zeros((GDN_DK, GDN_DV), F32)
    pairs = range(GDN_HEADS // 2)
    batches = range(nbatch)

    def wide(xa, xb):
        return jnp.concatenate([jnp.broadcast_to(xa, (C, GDN_DK)), jnp.broadcast_to(xb, (C, GDN_DK))], 1)

    def block_diag(x):
        return jnp.concatenate([jnp.where(left, x, 0.0), jnp.where(left, 0.0, x)], 0)

    def prepare(chunks, out):
        g0 = chunks[0] * C
        gcum_c, gcum_r, gb_grp = [], [], []
        for b in batches:
            gb_grp.append(gbcol_ref[b, g0:g0 + grows, :])
            gcum_c.append(_dot(cum_down, gb_grp[b], HIGHEST))
            gcum_r.append(_dot(gbrow_ref[b, :, g0:g0 + grows], cum_right, HIGHEST))
        low, attn, rhs, qdec, kdec, egl = [], [], [], [], [], []
        for ci, b, pr in [(ci, b, pr) for ci in chunks for b in batches for pr in pairs]:
            r0, n0, lo = ci * C, ci * C - g0, pr * 2 * GDN_DK
            ha, hb = 2 * pr, 2 * pr + 1
            gc_a = gcum_c[b][n0:n0 + C, G_LANE + ha:G_LANE + ha + 1]
            gc_b = gcum_c[b][n0:n0 + C, G_LANE + hb:G_LANE + hb + 1]
            gr_w = gcum_r[b][:, 2 * n0:2 * n0 + W]
            gr = jnp.where(left_row, gr_w[GDN_HEADS + ha:GDN_HEADS + ha + 1, :],
                           gr_w[GDN_HEADS + hb:GDN_HEADS + hb + 1, :])
            gl_a, gl_b = gc_a[C - 1:C, :], gc_b[C - 1:C, :]
            beta = wide(gb_grp[b][n0:n0 + C, BETA_LANE + ha:BETA_LANE + ha + 1],
                        gb_grp[b][n0:n0 + C, BETA_LANE + hb:BETA_LANE + hb + 1])
            decay = jnp.exp(jnp.where(tri_incl, jnp.where(left, gc_a, gc_b) - gr, -jnp.inf))
            eg = wide(jnp.exp(gc_a), jnp.exp(gc_b))
            qp = q_ref[b, r0:r0 + C, lo:lo + 2 * GDN_DK].astype(F32)
            kp = k_ref[b, r0:r0 + C, lo:lo + 2 * GDN_DK].astype(F32)
            vp = v_ref[b, r0:r0 + C, lo:lo + 2 * GDN_DV].astype(F32)
            kb = kp * beta
            k_bd = jnp.concatenate([jnp.concatenate([kp[:, :GDN_DK], zeros_c], 1),
                                    jnp.concatenate([zeros_c, kp[:, GDN_DK:]], 1)], 0)
            kq = _bdot_nt(jnp.concatenate([kb, qp], 0), k_bd)
            low.append(jnp.where(tri_strict, kq[:C] * decay, 0.0))
            attn.append(jnp.where(tri_incl, kq[C:] * decay, 0.0))
            vb, kbeg = vp * beta, kb * eg
            rhs.append(jnp.concatenate(
                [jnp.concatenate([vb[:, :GDN_DV], kbeg[:, :GDN_DK], zeros_c, zeros_c], 1),
                 jnp.concatenate([zeros_c, zeros_c, vb[:, GDN_DV:], kbeg[:, GDN_DK:]], 1)], 0))
            qdec.append(qp * eg)
            kdec.append(kp * wide(jnp.exp(gl_a - gc_a), jnp.exp(gl_b - gc_b)))
            egl.append((jnp.exp(gl_a), jnp.exp(gl_b)))
        yield
        items = range(len(low))
        inv = [eye - low[it] for it in items]
        pw = [_bdot(low[it], block_diag(low[it])) for it in items]
        yield
        span = 2
        while span < C:
            if 2 * span >= C:
                inv = [inv[it] + _bdot(inv[it], block_diag(pw[it])) for it in items]
            else:
                both = [_bdot(jnp.concatenate([inv[it], pw[it]], 0), block_diag(pw[it]))
                        for it in items]
                inv = [inv[it] + both[it][:C] for it in items]
                pw = [both[it][C:] for it in items]
            span *= 2
            yield
        sol = [_bdot(inv[it], rhs[it]) for it in items]
        per_chunk = nbatch * len(pairs)
        for n, ci in enumerate(chunks):
            sl = slice(n * per_chunk, (n + 1) * per_chunk)
            out[ci] = dict(
                u=[jnp.concatenate([x[:, :GDN_DV], x[:, 2 * GDN_DV:3 * GDN_DV]], 1) for x in sol[sl]],
                w=[jnp.concatenate([x[:, GDN_DV:2 * GDN_DV], x[:, 3 * GDN_DV:]], 1) for x in sol[sl]],
                attn=attn[sl], qdec=qdec[sl], kdec=kdec[sl], egl=egl[sl])
        yield

    def recur_chunk(ci, pre, s):
        r0 = ci * C
        items = [(b, pr) for b in batches for pr in pairs]
        s_bd = [jnp.concatenate([jnp.concatenate([s[b][2 * pr], zeros_s], 1),
                                 jnp.concatenate([zeros_s, s[b][2 * pr + 1]], 1)], 0)
                for b, pr in items]
        ws = [_bdot(jnp.concatenate([pre["w"][n], pre["qdec"][n]], 0), s_bd[n])
              for n in range(len(items))]
        yield
        v_new = [pre["u"][n] - ws[n][:C] for n in range(len(items))]
        v_bd = [jnp.concatenate([jnp.concatenate([v[:, :GDN_DV], zeros_c], 1),
                                 jnp.concatenate([zeros_c, v[:, GDN_DV:]], 1)], 0) for v in v_new]
        o = [ws[n][C:] + _bdot(pre["attn"][n], v_bd[n]) for n in range(len(items))]
        for n, (b, pr) in enumerate(items):
            for half in range(2):
                lo = half * GDN_DK
                s[b][2 * pr + half] = s[b][2 * pr + half] * pre["egl"][n][half] + _bdot_tn(
                    pre["kdec"][n][:, lo:lo + GDN_DK], v_new[n][:, lo:lo + GDN_DV])
        yield
        for n, (b, pr) in enumerate(items):
            for half in range(2):
                hd = 2 * pr + half
                oh = o[n][:, half * GDN_DV:(half + 1) * GDN_DV]
                zh = z_ref[b, r0:r0 + C, hd * GDN_DV:(hd + 1) * GDN_DV].astype(F32)
                on = oh * lax.rsqrt(jnp.mean(oh * oh, -1, keepdims=True) + RMS_EPS) * ng
                o_ref[b, r0:r0 + C, hd * GDN_DV:(hd + 1) * GDN_DV] = (on * _silu(zh)).astype(BF16)
        yield

    def recur(chunks, prepared, s):
        for ci in chunks:
            yield from recur_chunk(ci, prepared[ci], s)

    s = [[state[b, hd] for hd in range(GDN_HEADS)] for b in batches]
    groups = [list(range(g, min(g + group, nchunk))) for g in range(0, nchunk, group)]
    prepared = {}
    for _ in prepare(groups[0], prepared):
        pass
    for gi, chunks in enumerate(groups):
        nxt = prepare(groups[gi + 1], prepared) if gi + 1 < len(groups) else iter(())
        cur = recur(chunks, prepared, s)
        nxt_done = cur_done = False
        while not (nxt_done and cur_done):
            nxt_done = nxt_done or next(nxt, "end") == "end"
            cur_done = cur_done or next(cur, "end") == "end"
    for b in batches:
        for hd in range(GDN_HEADS):
            state[b, hd] = s[b][hd]


def _gdn_call(qg, kg, vg, zg, gbcol, gbrow, layer, norm_g, *, rows, group):
    B, S, _ = qg.shape
    row = lambda w: pl.BlockSpec((B, rows, w), lambda i: (0, i, 0))
    kern = functools.partial(_gdn_kernel, nchunk=rows // CHUNK, group=group)
    return pl.pallas_call(
        kern, out_shape=jax.ShapeDtypeStruct((B, S, GDN_VW), BF16), grid=(S // rows,),
        in_specs=[row(GDN_QK), row(GDN_QK), row(GDN_VW), row(GDN_VW), row(LANES),
                  pl.BlockSpec((B, 2 * GDN_HEADS, rows), lambda i: (0, 0, i)),
                  _layer_spec(norm_g, layer)],
        out_specs=row(GDN_VW),
        scratch_shapes=[pltpu.VMEM((B, GDN_HEADS, GDN_DK, GDN_DV), F32)],
        compiler_params=pltpu.CompilerParams(dimension_semantics=("arbitrary",),
                                             vmem_limit_bytes=V7X_VMEM_LIMIT),
        name="gdn",
    )(qg, kg, vg, zg, gbcol, gbrow, norm_g)


def _attn_kernel(qt_ref, k_ref, vt_ref, o_ref, m_ref, acc_ref, st_ref, *, blk, hps):
    i = pl.program_id(2)
    m_ref[...] = jnp.full_like(m_ref, -jnp.inf)
    acc_ref[...] = jnp.zeros_like(acc_ref)

    def scores(hd, j, masked=False):
        start = pl.multiple_of(j * blk, blk)
        s = _dot(k_ref[0, hd, pl.ds(start, blk), :], qt_ref[0, hd, 0]).astype(BF16)
        if masked:
            kc = lax.broadcasted_iota(jnp.int32, (blk, blk), 0) // CHUNK
            qc = lax.broadcasted_iota(jnp.int32, (blk, blk), 1) // CHUNK
            s = jnp.where(kc <= qc, s, -jnp.inf)
        st_ref[hd] = s

    def process(j, nxt):
        for hd in range(hps):
            s = st_ref[hd]
            if nxt is not None:
                scores(hd, nxt)
            m_old = m_ref[hd]
            m_new = jnp.maximum(m_old, jnp.max(s, axis=0, keepdims=True).astype(F32))
            alpha = jnp.exp2(m_old - m_new)
            p = jnp.exp2(s - m_new.astype(BF16))
            acc_ref[hd] = alpha * acc_ref[hd] + _dot(vt_ref[0, hd, j], p)
            m_ref[hd] = m_new

    for hd in range(hps):
        scores(hd, i, masked=True)
    process(i, 0)

    @pl.when(i > 0)
    def _():
        lax.fori_loop(0, i - 1, lambda j, carry: (process(j, j + 1), carry)[1], 0)
        process(i - 1, None)

    for hd in range(hps):
        out = acc_ref[hd, 0:MLA_V, :] / acc_ref[hd, MLA_V:MLA_V + 1, :]
        o_ref[0, :, hd * MLA_V:(hd + 1) * MLA_V] = out.T.astype(BF16)


def _attn_call(qt, ka, vt, *, blk, hps):
    B, H, S, _ = ka.shape
    nb = S // blk
    kern = functools.partial(_attn_kernel, blk=blk, hps=hps)
    return pl.pallas_call(
        kern, out_shape=jax.ShapeDtypeStruct((B, S, H * MLA_V), BF16), grid=(B, H // hps, nb),
        in_specs=[pl.BlockSpec((1, hps, 1, ATT_DK, blk), lambda b, h, i: (b, h, i, 0, 0)),
                  pl.BlockSpec((1, hps, S, ATT_DK), lambda b, h, i: (b, h, 0, 0)),
                  pl.BlockSpec((1, hps, nb, VT_ROWS, blk), lambda b, h, i: (b, h, 0, 0, 0),
                               pipeline_mode=pl.Buffered(1))],
        out_specs=pl.BlockSpec((1, blk, hps * MLA_V), lambda b, h, i: (b, i, h)),
        scratch_shapes=[pltpu.VMEM((hps, 1, blk), F32), pltpu.VMEM((hps, VT_ROWS, blk), F32),
                        pltpu.VMEM((hps, blk, blk), BF16)],
        compiler_params=pltpu.CompilerParams(
            dimension_semantics=("arbitrary", "arbitrary", "arbitrary"),
            vmem_limit_bytes=V7X_VMEM_LIMIT),
        name="attn",
    )(qt, ka, vt)


def _layer_norm(x, g, b):
    mu = jnp.mean(x, -1, keepdims=True)
    xc = x - mu
    var = jnp.mean(xc * xc, -1, keepdims=True)
    return xc * lax.rsqrt(var + LN_EPS) * g + b


def _post_kernel(x_ref, og_ref, om_ref, p_ref, wog_ref, wom_ref, l1g_ref, l1b_ref, wg_ref, wu_ref,
                 wd_ref, l2g_ref, l2b_ref, wple_ref, wpg_ref, o_ref, *, alpha, ff_chunk, streams):
    d_ff = wd_ref.shape[0]
    rows = x_ref.shape[0] // streams

    def stream(n):
        rs = slice(n * rows, (n + 1) * rows)
        mix = _dot(og_ref[rs, :], wog_ref[...]) + _dot(om_ref[rs, :], wom_ref[...])
        yield
        x1 = _layer_norm(alpha * x_ref[rs, :] + mix, l1g_ref[...], l1b_ref[...])
        x1b = x1.astype(BF16)
        yield
        ff = jnp.zeros_like(x1)
        for c0 in range(0, d_ff, ff_chunk):
            c1 = min(c0 + ff_chunk, d_ff)
            gate = _dot(x1b, wg_ref[:, c0:c1])
            up = _dot(x1b, wu_ref[:, c0:c1])
            ff = ff + _dot((_silu(gate) * up).astype(BF16), wd_ref[c0:c1, :])
            yield
        x2 = _layer_norm(alpha * x1 + ff, l2g_ref[...], l2b_ref[...])
        yield
        emb = _dot(p_ref[rs, :].astype(BF16), wple_ref[...])
        o_ref[rs, :] = x2 + _sigmoid(_dot(x2.astype(BF16), wpg_ref[...])) * emb
        yield

    live, started = [], 0
    while started < streams or live:
        if started < streams:
            live.append(stream(started))
            started += 1
        live = [g for g in live if next(g, "end") != "end"]


def _post_call(x2d, og, om, p3d, layer, w_out, ln1_g, ln1_b, w_gate_up, w_down, ln2_g, ln2_b, w_ple,
               w_pg, *, tm, alpha, ff_chunk):
    T, D = x2d.shape
    row = lambda w: pl.BlockSpec((tm, w), lambda i: (i, 0))
    weights = ((w_out, (0, 2, 0)), (w_out, (0, 2, 1)), (ln1_g, None), (ln1_b, None),
               (w_gate_up, (1, 2, 0)), (w_gate_up, (1, 2, 1)), (w_down, None), (ln2_g, None),
               (ln2_b, None), (w_ple, None), (w_pg, None))
    kern = functools.partial(_post_kernel, alpha=alpha, ff_chunk=ff_chunk, streams=2)
    return pl.pallas_call(
        kern, out_shape=jax.ShapeDtypeStruct((T, D), F32), grid=(T // tm,),
        in_specs=[row(D), row(og.shape[1]), row(om.shape[1]),
                  pl.BlockSpec((None, tm, p3d.shape[2]), lambda i: (layer, i, 0))]
        + [_layer_spec(w, layer, split) for w, split in weights],
        out_specs=row(D),
        compiler_params=pltpu.CompilerParams(dimension_semantics=("arbitrary",),
                                             vmem_limit_bytes=V7X_VMEM_LIMIT),
        name="post",
    )(x2d, og, om, p3d, *[w for w, _ in weights])


def _reorder_weights(w_in, w_uq, w_ukv, a_log, dt_bias):
    depth, d, _ = w_in.shape
    q, k, v, z, b, a, cq, ckv, kr = jnp.split(
        w_in.astype(BF16),
        np.cumsum((GDN_QK, GDN_QK, GDN_VW, GDN_VW, GDN_HEADS, GDN_HEADS, Q_LORA, KV_LORA)).tolist(),
        axis=-1)
    pad = jnp.zeros((depth, d, LANES - MLA_ROPE - 2 * GDN_HEADS), BF16)
    w_in_r = jnp.concatenate([q, k, v, z, cq, ckv, kr, b, a, pad], -1)

    uq = w_uq.reshape(depth, Q_LORA, MLA_HEADS, MLA_NOPE + MLA_ROPE)
    uq_nope = uq[..., :MLA_NOPE].reshape(depth, Q_LORA, MLA_HEADS * MLA_NOPE)
    lane_pad = ((0, 0), (0, 0), (0, 0), (0, LANES - MLA_ROPE))
    rope_cols = uq[..., MLA_NOPE:]
    partner_cols = jnp.concatenate([rope_cols[..., MLA_ROPE // 2:], rope_cols[..., :MLA_ROPE // 2]], -1)
    w_uq_r = jnp.concatenate(
        [uq_nope, jnp.pad(rope_cols, lane_pad).reshape(depth, Q_LORA, MLA_HEADS * LANES),
         jnp.pad(partner_cols, lane_pad).reshape(depth, Q_LORA, MLA_HEADS * LANES)], -1).astype(BF16)

    ukv = w_ukv.reshape(depth, KV_LORA, MLA_HEADS, MLA_NOPE + MLA_V)
    w_ukv_r = jnp.concatenate([ukv[..., :MLA_NOPE].reshape(depth, KV_LORA, -1),
                               ukv[..., MLA_NOPE:].reshape(depth, KV_LORA, -1)], -1).astype(BF16)

    lanes_pad = ((0, 0), (0, 0), (G_LANE, LANES - G_LANE - GDN_HEADS))
    alog_pad = jnp.pad(a_log[:, None, :], lanes_pad)
    dtb_pad = jnp.pad(dt_bias[:, None, :], lanes_pad)
    return w_in_r, w_uq_r, w_ukv_r, alog_pad, dtb_pad


def _rope_tables(positions):
    inv_freq = ROPE_THETA ** (-jnp.arange(0, MLA_ROPE, 2, dtype=F32) / MLA_ROPE)
    ang = positions.astype(F32)[..., None] * inv_freq
    cos, sin = jnp.cos(ang), jnp.sin(ang)
    reps = LANES // MLA_ROPE
    return (jnp.tile(jnp.concatenate([cos, cos], -1), (1, 1, reps)),
            jnp.tile(jnp.concatenate([-sin, sin], -1), (1, 1, reps)))


def _tile_rows(s, want):
    t = min(want, s)
    while s % t:
        t //= 2
    return t


def _plan(seq, d_ff):
    return dict(
        tm=_tile_rows(seq, 512),
        gdn_rows=_tile_rows(seq, 4 * CHUNK), gdn_group=2,
        ff_chunk=-(-d_ff // (2 * V7X_MXU_COLS)) * V7X_MXU_COLS,
    )


def kernel(x, p, positions, w_in, conv_w, a_log, dt_bias, gdn_norm_g, q_norm_g, w_uq, kv_norm_g,
           w_ukv, w_out, ln1_g, ln1_b, w_gate_up, w_down, ln2_g, ln2_b, w_ple, w_ple_gate):
    B, S, D = x.shape
    depth = w_in.shape[0]
    d_ff = w_down.shape[1]
    alpha = (2.0 * depth) ** 0.25
    plan = _plan(S, d_ff)
    tm = plan["tm"]

    w_in_r, w_uq_r, w_ukv_r, alog_pad, dtb_pad = _reorder_weights(w_in, w_uq, w_ukv, a_log, dt_bias)
    cos_t, sin_t = _rope_tables(positions)
    w_out_b = w_out.astype(BF16)
    w_gu_b = w_gate_up.astype(BF16)
    w_down_b = w_down.astype(BF16)
    w_ple_b = w_ple.astype(BF16)
    w_pg_b = w_ple_gate.astype(BF16)
    vec = lambda a: a[:, None, :]
    p3d = p.reshape(depth, B * S, -1)

    for i in range(depth):
        qg, kg, vg, zg, gbcol, gbrow, qt, ka, vt = _prep_call(
            x, i, w_in_r, conv_w, alog_pad, dtb_pad, vec(q_norm_g), vec(kv_norm_g), w_uq_r, w_ukv_r,
            cos_t, sin_t, tm=tm)
        og = _gdn_call(qg, kg, vg, zg, gbcol, gbrow, i, vec(gdn_norm_g), rows=plan["gdn_rows"],
                       group=plan["gdn_group"])
        om = _attn_call(qt, ka, vt, blk=tm, hps=MLA_HEADS)
        x = _post_call(
            x.reshape(B * S, D), og.reshape(B * S, -1), om.reshape(B * S, -1), p3d, i, w_out_b,
            vec(ln1_g), vec(ln1_b), w_gu_b, w_down_b, vec(ln2_g), vec(ln2_b), w_ple_b, w_pg_b,
            tm=tm, alpha=alpha, ff_chunk=plan["ff_chunk"],
        ).reshape(B, S, D)
    return x
```

```python
import functools

import jax
import jax.numpy as jnp
import numpy as np
from jax import lax
from jax.experimental import pallas as pl
from jax.experimental.pallas import tpu as pltpu

F32 = jnp.float32
BF16 = jnp.bfloat16

CHUNK = 64
GDN_HEADS = 4
GDN_DK = 128
GDN_DV = 128
CONV_WIDTH = 4
MLA_HEADS = 4
MLA_NOPE = 128
MLA_ROPE = 64
MLA_V = 128
Q_LORA = 384
KV_LORA = 256
ROPE_THETA = 10000.0
LN_EPS = 1e-5
RMS_EPS = 1e-6

GDN_QK = GDN_HEADS * GDN_DK
GDN_VW = GDN_HEADS * GDN_DV
QKV_W = 2 * GDN_QK + GDN_VW
LANES = 128
Z_OFF = QKV_W
CQ_OFF = Z_OFF + GDN_VW
CKV_OFF = CQ_OFF + Q_LORA
LAST_OFF = CKV_OFF + KV_LORA
IN_PAD = LAST_OFF + LANES
BETA_LANE = MLA_ROPE
G_LANE = MLA_ROPE + GDN_HEADS
ATT_DK = 2 * LANES
F32_SUBLANES = 8
BF16_SUBLANES = 16
VT_ROWS = MLA_V + BF16_SUBLANES
CONV_COLS = GDN_QK

V7X_VMEM_LIMIT = 56 * 1024 * 1024
V7X_MXU_COLS = 256

HIGHEST = lax.Precision.HIGHEST


def _dot(a, b, precision=None):
    return jnp.dot(a, b, preferred_element_type=F32, precision=precision)


def _dot_nt(a, b, precision=None):
    return lax.dot_general(a, b, (((1,), (1,)), ((), ())), preferred_element_type=F32,
                           precision=precision)


def _dot_tn(a, b, precision=None):
    return lax.dot_general(a, b, (((0,), (0,)), ((), ())), preferred_element_type=F32,
                           precision=precision)


def _bdot(a, b):
    return _dot(a.astype(BF16), b.astype(BF16))


def _bdot_nt(a, b):
    return _dot_nt(a.astype(BF16), b.astype(BF16))


def _bdot_tn(a, b):
    return _dot_tn(a.astype(BF16), b.astype(BF16))


def _sigmoid(x):
    return 1.0 / (1.0 + jnp.exp(-x))


def _silu(x):
    return x * _sigmoid(x)


def _softplus(x):
    return jnp.maximum(x, 0.0) + jnp.log(1.0 + jnp.exp(-jnp.abs(x)))


def _layer_spec(arr, layer, split=None):
    block = list(arr.shape[1:])
    index = [0] * len(block)
    if split is not None:
        axis, parts, which = split
        block[axis] //= parts
        index[axis] = which
    return pl.BlockSpec((None, *block), lambda *_: (layer, *index), pipeline_mode=pl.Buffered(1))


def _prep_kernel(x_ref, w_in_ref, conv_w_ref, alog_ref, dtb_ref, qng_ref, kvng_ref, w_uq_ref,
                 w_ukv_ref, cos_ref, sin_ref,
                 qg_ref, kg_ref, vg_ref, zg_ref, gbcol_ref, gbrow_ref, qt_ref, ka_ref, vt_ref,
                 cbuf, *, tm, att_scale, streams):
    @pl.when(pl.program_id(1) == 0)
    def _():
        cbuf[...] = jnp.zeros_like(cbuf)

    rows = tm // streams

    def stream(n):
        rs = slice(n * rows, (n + 1) * rows)
        top_rows = lax.broadcasted_iota(jnp.int32, (F32_SUBLANES, CONV_COLS), 0)

        xb = x_ref[0, rs].astype(BF16)
        lane = lax.broadcasted_iota(jnp.int32, (rows, LANES), 1)
        cos = cos_ref[0, rs]
        sin = sin_ref[0, rs]
        first_half = (lane % MLA_ROPE) < (MLA_ROPE // 2)
        nh = MLA_HEADS * MLA_NOPE

        def proj(c0, c1):
            return _dot(xb, w_in_ref[:, c0:c1])

        def rope(blk):
            partner = jnp.where(first_half, pltpu.roll(blk, LANES - MLA_ROPE // 2, 1),
                                pltpu.roll(blk, MLA_ROPE // 2, 1))
            return blk * cos + partner * sin

        def conv_group(grp, part, out_ref):
            c0 = grp * GDN_QK + part * CONV_COLS
            c1 = c0 + CONV_COLS
            o0 = part * CONV_COLS
            pre = proj(c0, c1)
            cw = conv_w_ref[:, c0:c1]
            y = pre * cw[0:1, :]
            for j in range(1, CONV_WIDTH):
                carry_in = cbuf[j - 1:j, c0:c1]
                cbuf[j - 1:j, c0:c1] = y[rows - 1:rows, :]
                shifted = pltpu.roll(y, 1, 0)
                top = jnp.where(top_rows == 0, carry_in, shifted[0:F32_SUBLANES, :])
                y = jnp.concatenate([top, shifted[F32_SUBLANES:, :]], 0) + pre * cw[j:j + 1, :]
            y = _silu(y)
            if out_ref is vg_ref:
                out_ref[0, rs, o0:o0 + CONV_COLS] = y.astype(BF16)
                return
            out_scale = GDN_DK ** -0.5 if out_ref is qg_ref else 1.0
            for lo in range(0, CONV_COLS, GDN_DK):
                yh = y[:, lo:lo + GDN_DK]
                yn = yh * (lax.rsqrt(jnp.sum(yh * yh, -1, keepdims=True) + RMS_EPS) * out_scale)
                out_ref[0, rs, o0 + lo:o0 + lo + GDN_DK] = yn.astype(BF16)

        for grp, out_ref in enumerate((qg_ref, kg_ref, vg_ref)):
            for part in range(GDN_QK // CONV_COLS):
                conv_group(grp, part, out_ref)
                yield

        latent = proj(CQ_OFF, IN_PAD)
        cq = latent[:, :Q_LORA]
        cqn = cq * lax.rsqrt(jnp.mean(cq * cq, -1, keepdims=True) + RMS_EPS) * qng_ref[...]
        qm = _dot(cqn.astype(BF16), w_uq_ref[...])
        for hd in range(MLA_HEADS):
            lo = hd * LANES
            q_rope = qm[:, nh + lo:nh + lo + LANES] * cos + qm[:, 2 * nh + lo:2 * nh + lo + LANES] * sin
            q_full = jnp.concatenate([qm[:, lo:lo + LANES], q_rope], -1) * att_scale
            qt_ref[0, hd, 0, :, rs] = q_full.T.astype(BF16)
        yield

        ckv = latent[:, Q_LORA:Q_LORA + KV_LORA]
        last = latent[:, Q_LORA + KV_LORA:]
        ckvn = ckv * lax.rsqrt(jnp.mean(ckv * ckv, -1, keepdims=True) + RMS_EPS) * kvng_ref[...]
        kvm = _dot(ckvn.astype(BF16), w_ukv_ref[...])
        k_rope = jnp.where(lane < MLA_ROPE, rope(last), 0.0)
        ones_rows = (lax.broadcasted_iota(jnp.int32, (VT_ROWS - MLA_V, rows), 0) == 0).astype(BF16)
        for hd in range(MLA_HEADS):
            lo = hd * LANES
            ka_ref[0, hd, rs] = jnp.concatenate([kvm[:, lo:lo + LANES], k_rope], -1).astype(BF16)
            vt_ref[0, hd, 0, 0:MLA_V, rs] = kvm[:, nh + lo:nh + lo + LANES].T.astype(BF16)
            vt_ref[0, hd, 0, MLA_V:VT_ROWS, rs] = ones_rows
        yield

        beta = _sigmoid(last)
        g = -jnp.exp(alog_ref[...]) * _softplus(last + dtb_ref[...])
        gb = jnp.where((lane >= BETA_LANE) & (lane < G_LANE), beta,
                       jnp.where((lane >= G_LANE) & (lane < G_LANE + GDN_HEADS), g, 0.0))
        gbcol_ref[0, rs] = gb
        gbrow_ref[0, :, rs] = gb.T[BETA_LANE:BETA_LANE + 2 * GDN_HEADS, :]

        zg_ref[0, rs] = proj(Z_OFF, Z_OFF + GDN_VW).astype(BF16)
        yield

    live, started = [], 0
    while started < streams or live:
        if started < streams:
            live.append(stream(started))
            started += 1
        live = [g for g in live if next(g, "end") != "end"]


def _prep_call(x, layer, w_in_r, conv_w, alog_pad, dtb_pad, qn_g, kvn_g, w_uq_r, w_ukv_r, cos_t,
               sin_t, *, tm):
    B, S, D = x.shape
    params = (w_in_r, conv_w, alog_pad, dtb_pad, qn_g, kvn_g, w_uq_r, w_ukv_r)
    nt = S // tm
    row = lambda w: pl.BlockSpec((1, tm, w), lambda b, i: (b, i, 0))
    head = pl.BlockSpec((1, MLA_HEADS, tm, ATT_DK), lambda b, i: (b, 0, i, 0))
    out_shape = (
        jax.ShapeDtypeStruct((B, S, GDN_QK), BF16), jax.ShapeDtypeStruct((B, S, GDN_QK), BF16),
        jax.ShapeDtypeStruct((B, S, GDN_VW), BF16), jax.ShapeDtypeStruct((B, S, GDN_VW), BF16),
        jax.ShapeDtypeStruct((B, S, LANES), F32), jax.ShapeDtypeStruct((B, 2 * GDN_HEADS, S), F32),
        jax.ShapeDtypeStruct((B, MLA_HEADS, nt, ATT_DK, tm), BF16),
        jax.ShapeDtypeStruct((B, MLA_HEADS, S, ATT_DK), BF16),
        jax.ShapeDtypeStruct((B, MLA_HEADS, nt, VT_ROWS, tm), BF16),
    )
    transposed = lambda r: pl.BlockSpec((1, MLA_HEADS, 1, r, tm), lambda b, i: (b, 0, i, 0, 0))
    out_specs = (
        row(GDN_QK), row(GDN_QK), row(GDN_VW), row(GDN_VW), row(LANES),
        pl.BlockSpec((1, 2 * GDN_HEADS, tm), lambda b, i: (b, 0, i)),
        transposed(ATT_DK), head, transposed(VT_ROWS),
    )
    in_specs = [row(D)] + [_layer_spec(a, layer) for a in params] + [row(LANES), row(LANES)]
    kern = functools.partial(_prep_kernel, tm=tm, streams=2,
                             att_scale=(MLA_NOPE + MLA_ROPE) ** -0.5 * float(np.log2(np.e)))
    return pl.pallas_call(
        kern, out_shape=out_shape, grid=(B, nt), in_specs=in_specs, out_specs=out_specs,
        scratch_shapes=[pltpu.VMEM((F32_SUBLANES, QKV_W), F32)],
        compiler_params=pltpu.CompilerParams(dimension_semantics=("arbitrary", "arbitrary"),
                                             vmem_limit_bytes=V7X_VMEM_LIMIT),
        name="prep",
    )(x, *params, cos_t, sin_t)


def _gdn_kernel(q_ref, k_ref, v_ref, z_ref, gbcol_ref, gbrow_ref, ng_ref, o_ref, state, *, nchunk,
                group):
    @pl.when(pl.program_id(0) == 0)
    def _():
        state[...] = jnp.zeros_like(state)

    nbatch = q_ref.shape[0]
    C = CHUNK
    W = 2 * C
    row = lax.broadcasted_iota(jnp.int32, (C, W), 0)
    lane = lax.broadcasted_iota(jnp.int32, (C, W), 1)
    left = lane < C
    col = lane % C
    tri_incl = row >= col
    tri_strict = row > col
    eye = (row == col).astype(F32)
    left_row = left[0:1, :]
    grows = group * C
    rr = lax.broadcasted_iota(jnp.int32, (grows, grows), 0)
    cc = lax.broadcasted_iota(jnp.int32, (grows, grows), 1)
    cum_down = (((rr // C) == (cc // C)) & (rr >= cc)).astype(F32)
    rr2 = lax.broadcasted_iota(jnp.int32, (grows, 2 * grows), 0)
    cc2 = lax.broadcasted_iota(jnp.int32, (grows, 2 * grows), 1)
    cum_right = (((rr2 // C) == (cc2 // W)) & (rr2 <= (cc2 // W) * C + cc2 % C)).astype(F32)
    ng = ng_ref[...]
    zeros_c = jnp.zeros((C, GDN_DK), F32)
    zeros_s = jnp.zeros((GDN_DK, GDN_DV), F32)
    pairs = range(GDN_HEADS // 2)
    batches = range(nbatch)

    def wide(xa, xb):
        return jnp.concatenate([jnp.broadcast_to(xa, (C, GDN_DK)), jnp.broadcast_to(xb, (C, GDN_DK))], 1)

    def block_diag(x):
        return jnp.concatenate([jnp.where(left, x, 0.0), jnp.where(left, 0.0, x)], 0)

    def prepare(chunks, out):
        g0 = chunks[0] * C
        gcum_c, gcum_r, gb_grp = [], [], []
        for b in batches:
            gb_grp.append(gbcol_ref[b, g0:g0 + grows, :])
            gcum_c.append(_dot(cum_down, gb_grp[b], HIGHEST))
            gcum_r.append(_dot(gbrow_ref[b, :, g0:g0 + grows], cum_right, HIGHEST))
        low, attn, rhs, qdec, kdec, egl = [], [], [], [], [], []
        for ci, b, pr in [(ci, b, pr) for ci in chunks for b in batches for pr in pairs]:
            r0, n0, lo = ci * C, ci * C - g0, pr * 2 * GDN_DK
            ha, hb = 2 * pr, 2 * pr + 1
            gc_a = gcum_c[b][n0:n0 + C, G_LANE + ha:G_LANE + ha + 1]
            gc_b = gcum_c[b][n0:n0 + C, G_LANE + hb:G_LANE + hb + 1]
            gr_w = gcum_r[b][:, 2 * n0:2 * n0 + W]
            gr = jnp.where(left_row, gr_w[GDN_HEADS + ha:GDN_HEADS + ha + 1, :],
                           gr_w[GDN_HEADS + hb:GDN_HEADS + hb + 1, :])
            gl_a, gl_b = gc_a[C - 1:C, :], gc_b[C - 1:C, :]
            beta = wide(gb_grp[b][n0:n0 + C, BETA_LANE + ha:BETA_LANE + ha + 1],
                        gb_grp[b][n0:n0 + C, BETA_LANE + hb:BETA_LANE + hb + 1])
            decay = jnp.exp(jnp.where(tri_incl, jnp.where(left, gc_a, gc_b) - gr, -jnp.inf))
            eg = wide(jnp.exp(gc_a), jnp.exp(gc_b))
            qp = q_ref[b, r0:r0 + C, lo:lo + 2 * GDN_DK].astype(F32)
            kp = k_ref[b, r0:r0 + C, lo:lo + 2 * GDN_DK].astype(F32)
            vp = v_ref[b, r0:r0 + C, lo:lo + 2 * GDN_DV].astype(F32)
            kb = kp * beta
            k_bd = jnp.concatenate([jnp.concatenate([kp[:, :GDN_DK], zeros_c], 1),
                                    jnp.concatenate([zeros_c, kp[:, GDN_DK:]], 1)], 0)
            kq = _bdot_nt(jnp.concatenate([kb, qp], 0), k_bd)
            low.append(jnp.where(tri_strict, kq[:C] * decay, 0.0))
            attn.append(jnp.where(tri_incl, kq[C:] * decay, 0.0))
            vb, kbeg = vp * beta, kb * eg
            rhs.append(jnp.concatenate(
                [jnp.concatenate([vb[:, :GDN_DV], kbeg[:, :GDN_DK], zeros_c, zeros_c], 1),
                 jnp.concatenate([zeros_c, zeros_c, vb[:, GDN_DV:], kbeg[:, GDN_DK:]], 1)], 0))
            qdec.append(qp * eg)
            kdec.append(kp * wide(jnp.exp(gl_a - gc_a), jnp.exp(gl_b - gc_b)))
            egl.append((jnp.exp(gl_a), jnp.exp(gl_b)))
        yield
        items = range(len(low))
        inv = [eye - low[it] for it in items]
        pw = [_bdot(low[it], block_diag(low[it])) for it in items]
        yield
        span = 2
        while span < C:
            if 2 * span >= C:
                inv = [inv[it] + _bdot(inv[it], block_diag(pw[it])) for it in items]
            else:
                both = [_bdot(jnp.concatenate([inv[it], pw[it]], 0), block_diag(pw[it]))
                        for it in items]
                inv = [inv[it] + both[it][:C] for it in items]
                pw = [both[it][C:] for it in items]
            span *= 2
            yield
        sol = [_bdot(inv[it], rhs[it]) for it in items]
        per_chunk = nbatch * len(pairs)
        for n, ci in enumerate(chunks):
            sl = slice(n * per_chunk, (n + 1) * per_chunk)
            out[ci] = dict(
                u=[jnp.concatenate([x[:, :GDN_DV], x[:, 2 * GDN_DV:3 * GDN_DV]], 1) for x in sol[sl]],
                w=[jnp.concatenate([x[:, GDN_DV:2 * GDN_DV], x[:, 3 * GDN_DV:]], 1) for x in sol[sl]],
                attn=attn[sl], qdec=qdec[sl], kdec=kdec[sl], egl=egl[sl])
        yield

    def recur_chunk(ci, pre, s):
        r0 = ci * C
        items = [(b, pr) for b in batches for pr in pairs]
        s_bd = [jnp.concatenate([jnp.concatenate([s[b][2 * pr], zeros_s], 1),
                                 jnp.concatenate([zeros_s, s[b][2 * pr + 1]], 1)], 0)
                for b, pr in items]
        ws = [_bdot(jnp.concatenate([pre["w"][n], pre["qdec"][n]], 0), s_bd[n])
              for n in range(len(items))]
        yield
        v_new = [pre["u"][n] - ws[n][:C] for n in range(len(items))]
        v_bd = [jnp.concatenate([jnp.concatenate([v[:, :GDN_DV], zeros_c], 1),
                                 jnp.concatenate([zeros_c, v[:, GDN_DV:]], 1)], 0) for v in v_new]
        o = [ws[n][C:] + _bdot(pre["attn"][n], v_bd[n]) for n in range(len(items))]
        for n, (b, pr) in enumerate(items):
            for half in range(2):
                lo = half * GDN_DK
                s[b][2 * pr + half] = s[b][2 * pr + half] * pre["egl"][n][half] + _bdot_tn(
                    pre["kdec"][n][:, lo:lo + GDN_DK], v_new[n][:, lo:lo + GDN_DV])
        yield
        for n, (b, pr) in enumerate(items):
            for half in range(2):
                hd = 2 * pr + half
                oh = o[n][:, half * GDN_DV:(half + 1) * GDN_DV]
                zh = z_ref[b, r0:r0 + C, hd * GDN_DV:(hd + 1) * GDN_DV].astype(F32)
                on = oh * lax.rsqrt(jnp.mean(oh * oh, -1, keepdims=True) + RMS_EPS) * ng
                o_ref[b, r0:r0 + C, hd * GDN_DV:(hd + 1) * GDN_DV] = (on * _silu(zh)).astype(BF16)
        yield

    def recur(chunks, prepared, s):
        for ci in chunks:
            yield from recur_chunk(ci, prepared[ci], s)

    s = [[state[b, hd] for hd in range(GDN_HEADS)] for b in batches]
    groups = [list(range(g, min(g + group, nchunk))) for g in range(0, nchunk, group)]
    prepared = {}
    for _ in prepare(groups[0], prepared):
        pass
    for gi, chunks in enumerate(groups):
        nxt = prepare(groups[gi + 1], prepared) if gi + 1 < len(groups) else iter(())
        cur = recur(chunks, prepared, s)
        nxt_done = cur_done = False
        while not (nxt_done and cur_done):
            nxt_done = nxt_done or next(nxt, "end") == "end"
            cur_done = cur_done or next(cur, "end") == "end"
    for b in batches:
        for hd in range(GDN_HEADS):
            state[b, hd] = s[b][hd]


def _gdn_call(qg, kg, vg, zg, gbcol, gbrow, layer, norm_g, *, rows, group):
    B, S, _ = qg.shape
    row = lambda w: pl.BlockSpec((B, rows, w), lambda i: (0, i, 0))
    kern = functools.partial(_gdn_kernel, nchunk=rows // CHUNK, group=group)
    return pl.pallas_call(
        kern, out_shape=jax.ShapeDtypeStruct((B, S, GDN_VW), BF16), grid=(S // rows,),
        in_specs=[row(GDN_QK), row(GDN_QK), row(GDN_VW), row(GDN_VW), row(LANES),
                  pl.BlockSpec((B, 2 * GDN_HEADS, rows), lambda i: (0, 0, i)),
                  _layer_spec(norm_g, layer)],
        out_specs=row(GDN_VW),
        scratch_shapes=[pltpu.VMEM((B, GDN_HEADS, GDN_DK, GDN_DV), F32)],
        compiler_params=pltpu.CompilerParams(dimension_semantics=("arbitrary",),
                                             vmem_limit_bytes=V7X_VMEM_LIMIT),
        name="gdn",
    )(qg, kg, vg, zg, gbcol, gbrow, norm_g)


def _attn_kernel(qt_ref, k_ref, vt_ref, o_ref, m_ref, acc_ref, st_ref, *, blk, hps):
    i = pl.program_id(2)
    m_ref[...] = jnp.full_like(m_ref, -jnp.inf)
    acc_ref[...] = jnp.zeros_like(acc_ref)

    def scores(hd, j, masked=False):
        start = pl.multiple_of(j * blk, blk)
        s = _dot(k_ref[0, hd, pl.ds(start, blk), :], qt_ref[0, hd, 0]).astype(BF16)
        if masked:
            kc = lax.broadcasted_iota(jnp.int32, (blk, blk), 0) // CHUNK
            qc = lax.broadcasted_iota(jnp.int32, (blk, blk), 1) // CHUNK
            s = jnp.where(kc <= qc, s, -jnp.inf)
        st_ref[hd] = s

    def process(j, nxt):
        for hd in range(hps):
            s = st_ref[hd]
            if nxt is not None:
                scores(hd, nxt)
            m_old = m_ref[hd]
            m_new = jnp.maximum(m_old, jnp.max(s, axis=0, keepdims=True).astype(F32))
            alpha = jnp.exp2(m_old - m_new)
            p = jnp.exp2(s - m_new.astype(BF16))
            acc_ref[hd] = alpha * acc_ref[hd] + _dot(vt_ref[0, hd, j], p)
            m_ref[hd] = m_new

    for hd in range(hps):
        scores(hd, i, masked=True)
    process(i, 0)

    @pl.when(i > 0)
    def _():
        lax.fori_loop(0, i - 1, lambda j, carry: (process(j, j + 1), carry)[1], 0)
        process(i - 1, None)

    for hd in range(hps):
        out = acc_ref[hd, 0:MLA_V, :] / acc_ref[hd, MLA_V:MLA_V + 1, :]
        o_ref[0, :, hd * MLA_V:(hd + 1) * MLA_V] = out.T.astype(BF16)


def _attn_call(qt, ka, vt, *, blk, hps):
    B, H, S, _ = ka.shape
    nb = S // blk
    kern = functools.partial(_attn_kernel, blk=blk, hps=hps)
    return pl.pallas_call(
        kern, out_shape=jax.ShapeDtypeStruct((B, S, H * MLA_V), BF16), grid=(B, H // hps, nb),
        in_specs=[pl.BlockSpec((1, hps, 1, ATT_DK, blk), lambda b, h, i: (b, h, i, 0, 0)),
                  pl.BlockSpec((1, hps, S, ATT_DK), lambda b, h, i: (b, h, 0, 0)),
                  pl.BlockSpec((1, hps, nb, VT_ROWS, blk), lambda b, h, i: (b, h, 0, 0, 0),
                               pipeline_mode=pl.Buffered(1))],
        out_specs=pl.BlockSpec((1, blk, hps * MLA_V), lambda b, h, i: (b, i, h)),
        scratch_shapes=[pltpu.VMEM((hps, 1, blk), F32), pltpu.VMEM((hps, VT_ROWS, blk), F32),
                        pltpu.VMEM((hps, blk, blk), BF16)],
        compiler_params=pltpu.CompilerParams(
            dimension_semantics=("arbitrary", "arbitrary", "arbitrary"),
            vmem_limit_bytes=V7X_VMEM_LIMIT),
        name="attn",
    )(qt, ka, vt)


def _layer_norm(x, g, b):
    mu = jnp.mean(x, -1, keepdims=True)
    xc = x - mu
    var = jnp.mean(xc * xc, -1, keepdims=True)
    return xc * lax.rsqrt(var + LN_EPS) * g + b


def _post_kernel(x_ref, og_ref, om_ref, p_ref, wog_ref, wom_ref, l1g_ref, l1b_ref, wg_ref, wu_ref,
                 wd_ref, l2g_ref, l2b_ref, wple_ref, wpg_ref, o_ref, *, alpha, ff_chunk, streams):
    d_ff = wd_ref.shape[0]
    rows = x_ref.shape[0] // streams

    def stream(n):
        rs = slice(n * rows, (n + 1) * rows)
        mix = _dot(og_ref[rs, :], wog_ref[...]) + _dot(om_ref[rs, :], wom_ref[...])
        yield
        x1 = _layer_norm(alpha * x_ref[rs, :] + mix, l1g_ref[...], l1b_ref[...])
        x1b = x1.astype(BF16)
        yield
        ff = jnp.zeros_like(x1)
        for c0 in range(0, d_ff, ff_chunk):
            c1 = min(c0 + ff_chunk, d_ff)
            gate = _dot(x1b, wg_ref[:, c0:c1])
            up = _dot(x1b, wu_ref[:, c0:c1])
            ff = ff + _dot((_silu(gate) * up).astype(BF16), wd_ref[c0:c1, :])
            yield
        x2 = _layer_norm(alpha * x1 + ff, l2g_ref[...], l2b_ref[...])
        yield
        emb = _dot(p_ref[rs, :].astype(BF16), wple_ref[...])
        o_ref[rs, :] = x2 + _sigmoid(_dot(x2.astype(BF16), wpg_ref[...])) * emb
        yield

    live, started = [], 0
    while started < streams or live:
        if started < streams:
            live.append(stream(started))
            started += 1
        live = [g for g in live if next(g, "end") != "end"]


def _post_call(x2d, og, om, p3d, layer, w_out, ln1_g, ln1_b, w_gate_up, w_down, ln2_g, ln2_b, w_ple,
               w_pg, *, tm, alpha, ff_chunk):
    T, D = x2d.shape
    row = lambda w: pl.BlockSpec((tm, w), lambda i: (i, 0))
    weights = ((w_out, (0, 2, 0)), (w_out, (0, 2, 1)), (ln1_g, None), (ln1_b, None),
               (w_gate_up, (1, 2, 0)), (w_gate_up, (1, 2, 1)), (w_down, None), (ln2_g, None),
               (ln2_b, None), (w_ple, None), (w_pg, None))
    kern = functools.partial(_post_kernel, alpha=alpha, ff_chunk=ff_chunk, streams=2)
    return pl.pallas_call(
        kern, out_shape=jax.ShapeDtypeStruct((T, D), F32), grid=(T // tm,),
        in_specs=[row(D), row(og.shape[1]), row(om.shape[1]),
                  pl.BlockSpec((None, tm, p3d.shape[2]), lambda i: (layer, i, 0))]
        + [_layer_spec(w, layer, split) for w, split in weights],
        out_specs=row(D),
        compiler_params=pltpu.CompilerParams(dimension_semantics=("arbitrary",),
                                             vmem_limit_bytes=V7X_VMEM_LIMIT),
        name="post",
    )(x2d, og, om, p3d, *[w for w, _ in weights])


def _reorder_weights(w_in, w_uq, w_ukv, a_log, dt_bias):
    depth, d, _ = w_in.shape
    q, k, v, z, b, a, cq, ckv, kr = jnp.split(
        w_in.astype(BF16),
        np.cumsum((GDN_QK, GDN_QK, GDN_VW, GDN_VW, GDN_HEADS, GDN_HEADS, Q_LORA, KV_LORA)).tolist(),
        axis=-1)
    pad = jnp.zeros((depth, d, LANES - MLA_ROPE - 2 * GDN_HEADS), BF16)
    w_in_r = jnp.concatenate([q, k, v, z, cq, ckv, kr, b, a, pad], -1)

    uq = w_uq.reshape(depth, Q_LORA, MLA_HEADS, MLA_NOPE + MLA_ROPE)
    uq_nope = uq[..., :MLA_NOPE].reshape(depth, Q_LORA, MLA_HEADS * MLA_NOPE)
    lane_pad = ((0, 0), (0, 0), (0, 0), (0, LANES - MLA_ROPE))
    rope_cols = uq[..., MLA_NOPE:]
    partner_cols = jnp.concatenate([rope_cols[..., MLA_ROPE // 2:], rope_cols[..., :MLA_ROPE // 2]], -1)
    w_uq_r = jnp.concatenate(
        [uq_nope, jnp.pad(rope_cols, lane_pad).reshape(depth, Q_LORA, MLA_HEADS * LANES),
         jnp.pad(partner_cols, lane_pad).reshape(depth, Q_LORA, MLA_HEADS * LANES)], -1).astype(BF16)

    ukv = w_ukv.reshape(depth, KV_LORA, MLA_HEADS, MLA_NOPE + MLA_V)
    w_ukv_r = jnp.concatenate([ukv[..., :MLA_NOPE].reshape(depth, KV_LORA, -1),
                               ukv[..., MLA_NOPE:].reshape(depth, KV_LORA, -1)], -1).astype(BF16)

    lanes_pad = ((0, 0), (0, 0), (G_LANE, LANES - G_LANE - GDN_HEADS))
    alog_pad = jnp.pad(a_log[:, None, :], lanes_pad)
    dtb_pad = jnp.pad(dt_bias[:, None, :], lanes_pad)
    return w_in_r, w_uq_r, w_ukv_r, alog_pad, dtb_pad


def _rope_tables(positions):
    inv_freq = ROPE_THETA ** (-jnp.arange(0, MLA_ROPE, 2, dtype=F32) / MLA_ROPE)
    ang = positions.astype(F32)[..., None] * inv_freq
    cos, sin = jnp.cos(ang), jnp.sin(ang)
    reps = LANES // MLA_ROPE
    return (jnp.tile(jnp.concatenate([cos, cos], -1), (1, 1, reps)),
            jnp.tile(jnp.concatenate([-sin, sin], -1), (1, 1, reps)))


def _tile_rows(s, want):
    t = min(want, s)
    while s % t:
        t //= 2
    return t


def _plan(seq, d_ff):
    return dict(
        tm=_tile_rows(seq, 512),
        gdn_rows=_tile_rows(seq, 4 * CHUNK), gdn_group=2,
        ff_chunk=-(-d_ff // (2 * V7X_MXU_COLS)) * V7X_MXU_COLS,
    )


def kernel(x, p, positions, w_in, conv_w, a_log, dt_bias, gdn_norm_g, q_norm_g, w_uq, kv_norm_g,
           w_ukv, w_out, ln1_g, ln1_b, w_gate_up, w_down, ln2_g, ln2_b, w_ple, w_ple_gate):
    B, S, D = x.shape
    depth = w_in.shape[0]
    d_ff = w_down.shape[1]
    alpha = (2.0 * depth) ** 0.25
    plan = _plan(S, d_ff)
    tm = plan["tm"]

    w_in_r, w_uq_r, w_ukv_r, alog_pad, dtb_pad = _reorder_weights(w_in, w_uq, w_ukv, a_log, dt_bias)
    cos_t, sin_t = _rope_tables(positions)
    w_out_b = w_out.astype(BF16)
    w_gu_b = w_gate_up.astype(BF16)
    w_down_b = w_down.astype(BF16)
    w_ple_b = w_ple.astype(BF16)
    w_pg_b = w_ple_gate.astype(BF16)
    vec = lambda a: a[:, None, :]
    p3d = p.reshape(depth, B * S, -1)

    for i in range(depth):
        qg, kg, vg, zg, gbcol, gbrow, qt, ka, vt = _prep_call(
            x, i, w_in_r, conv_w, alog_pad, dtb_pad, vec(q_norm_g), vec(kv_norm_g), w_uq_r, w_ukv_r,
            cos_t, sin_t, tm=tm)
        og = _gdn_call(qg, kg, vg, zg, gbcol, gbrow, i, vec(gdn_norm_g), rows=plan["gdn_rows"],
                       group=plan["gdn_group"])
        om = _attn_call(qt, ka, vt, blk=tm, hps=MLA_HEADS)
        x = _post_call(
            x.reshape(B * S, D), og.reshape(B * S, -1), om.reshape(B * S, -1), p3d, i, w_out_b,
            vec(ln1_g), vec(ln1_b), w_gu_b, w_down_b, vec(ln2_g), vec(ln2_b), w_ple_b, w_pg_b,
            tm=tm, alpha=alpha, ff_chunk=plan["ff_chunk"],
        ).reshape(B, S, D)
    return x
```
